```python
import jax
import jax.numpy as jnp
from jax import lax
import numpy as np

D_MODEL = 2048
BATCH = 16
SEQ = 2048
DEPTH = 2

GRID_W = 64
CTX_LEN = 256
HEAD_DIM = 128
BRANCH_W = D_MODEL // 2
NA_HEADS = BRANCH_W // HEAD_DIM
GQA_HEADS = BRANCH_W // HEAD_DIM
GQA_KV_HEADS = GQA_HEADS // 4
KV_W = GQA_KV_HEADS * HEAD_DIM
CONV_K = 3
MLP_HIDDEN = 4 * D_MODEL
N_BRANCH = 3
WIN_R = 8
WIN_C = 16
Q_BLOCK = 128
ROPE_THETA = 10000.0
NORM_EPS = 1e-6
NEG_INF = -1e30
ATTN_SCALE = HEAD_DIM ** -0.5
DN_ALPHA = (2 * DEPTH) ** 0.25
DN_BETA = (8 * DEPTH) ** -0.25

QA0 = 0
QB0 = QA0 + BRANCH_W
KA0 = QB0 + BRANCH_W
VA0 = KA0 + BRANCH_W
KB0 = VA0 + BRANCH_W
VB0 = KB0 + KV_W
CB0 = VB0 + KV_W
CC0 = CB0 + BRANCH_W
CX0 = CC0 + BRANCH_W
G0 = CX0 + BRANCH_W
IN_W = G0 + N_BRANCH * D_MODEL

kernel_name = 'hybrid_na_gqa_shortconv_dit_block'


def layer_norm(x):
    xf = x.astype(jnp.float32)
    mu = jnp.mean(xf, axis=-1, keepdims=True)
    var = jnp.mean(jnp.square(xf - mu), axis=-1, keepdims=True)
    return ((xf - mu) * lax.rsqrt(var + NORM_EPS)).astype(x.dtype)


def modulate(x, shift, scale):
    return layer_norm(x) * (1.0 + scale) + shift


def rms_norm(x, gain):
    xf = x.astype(jnp.float32)
    y = xf * lax.rsqrt(jnp.mean(jnp.square(xf), axis=-1, keepdims=True) + NORM_EPS)
    return y.astype(x.dtype) * gain


def split_heads(z, n_heads):
    b, n, _ = z.shape
    return z.reshape(b, n, n_heads, HEAD_DIM).transpose(0, 2, 1, 3)


def merge_heads(o):
    b, h, n, d = o.shape
    return o.transpose(0, 2, 1, 3).reshape(b, n, h * d)


def axial_rope_angles(n_tokens):
    t = jnp.arange(n_tokens)
    row = (t // GRID_W).astype(jnp.float32)
    col = (t % GRID_W).astype(jnp.float32)
    axis_dim = HEAD_DIM // 2
    freqs = ROPE_THETA ** (-jnp.arange(0, axis_dim, 2, dtype=jnp.float32) / axis_dim)
    return row[:, None] * freqs, col[:, None] * freqs


def rotate(x, ang):
    half = x.shape[-1] // 2
    x1, x2 = x[..., :half], x[..., half:]
    cos = jnp.cos(ang).astype(x.dtype)
    sin = jnp.sin(ang).astype(x.dtype)
    return jnp.concatenate([x1 * cos - x2 * sin, x1 * sin + x2 * cos], axis=-1)


def apply_axial_rope(x, ang_row, ang_col):
    half = HEAD_DIM // 2
    return jnp.concatenate([rotate(x[..., :half], ang_row), rotate(x[..., half:], ang_col)], axis=-1)


def grouped_attention(q, k, v):
    b, hq, n, d = q.shape
    hkv = k.shape[1]
    qg = q.reshape(b, hkv, hq // hkv, n, d)
    s = jnp.einsum('bkgqd,bksd->bkgqs', qg, k).astype(jnp.float32) * ATTN_SCALE
    p = jax.nn.softmax(s, axis=-1).astype(v.dtype)
    return jnp.einsum('bkgqs,bksd->bkgqd', p, v).reshape(b, hq, n, d)


def blocked_latent_attention(q, k, v, k_ctx, v_ctx):
    b, hq, s_len, d = q.shape
    hkv = k.shape[1]
    g = hq // hkv
    nb = s_len // Q_BLOCK
    k_all = jnp.concatenate([k, k_ctx], axis=2)
    v_all = jnp.concatenate([v, v_ctx], axis=2)
    qb = q.reshape(b, hkv, g, nb, Q_BLOCK, d).transpose(3, 0, 1, 2, 4, 5)

    def one_block(q_blk):
        sc = jnp.einsum('bkgqd,bksd->bkgqs', q_blk, k_all).astype(jnp.float32) * ATTN_SCALE
        p = jax.nn.softmax(sc, axis=-1).astype(v_all.dtype)
        return jnp.einsum('bkgqs,bksd->bkgqd', p, v_all)

    o = lax.map(one_block, qb)
    return o.transpose(1, 2, 3, 0, 4, 5).reshape(b, hq, s_len, d)


def neighbourhood_attention(q, k, v, k_ctx, v_ctx, rpb):
    b, h, s_len, d = q.shape
    rows = s_len // GRID_W
    wr = min(WIN_R, rows)
    r = jnp.arange(rows)
    row_start = jnp.clip(r - wr // 2, 0, rows - wr)
    key_rows = row_start[:, None] + jnp.arange(wr)[None, :]
    col = jnp.arange(GRID_W)
    col_start = jnp.clip(col - WIN_C // 2, 0, GRID_W - WIN_C)
    in_win = (col[None, :] >= col_start[:, None]) & (col[None, :] < col_start[:, None] + WIN_C)
    dr_idx = key_rows - r[:, None] + WIN_R - 1
    dc_idx = jnp.clip(col[None, :] - col[:, None], -(WIN_C - 1), WIN_C - 1) + WIN_C - 1
    bias = rpb[:, dr_idx[:, None, :, None], dc_idx[None, :, None, :]].astype(jnp.float32)
    bias = jnp.where(in_win[None, None, :, None, :], bias, NEG_INF)
    qg = q.reshape(b, h, rows, GRID_W, d)
    kg = k.reshape(b, h, rows, GRID_W, d)[:, :, key_rows]
    vg = v.reshape(b, h, rows, GRID_W, d)[:, :, key_rows]
    s_loc = jnp.einsum('bhrqd,bhrwkd->bhrqwk', qg, kg).astype(jnp.float32) * ATTN_SCALE + bias[None]
    n_loc = wr * GRID_W
    s_loc = s_loc.reshape(b, h, rows, GRID_W, n_loc)
    s_ctx = jnp.einsum('bhrqd,bhld->bhrql', qg, k_ctx).astype(jnp.float32) * ATTN_SCALE
    p = jax.nn.softmax(jnp.concatenate([s_loc, s_ctx], axis=-1), axis=-1).astype(v.dtype)
    p_loc = p[..., :n_loc].reshape(b, h, rows, GRID_W, wr, GRID_W)
    o = (jnp.einsum('bhrqwk,bhrwkd->bhrqd', p_loc, vg)
         + jnp.einsum('bhrql,bhld->bhrqd', p[..., n_loc:], v_ctx))
    return o.reshape(b, h, s_len, d)


def short_conv(u, w):
    return lax.conv_general_dilated(
        u, w[:, None, :], window_strides=(1,), padding=[(CONV_K // 2, CONV_K // 2)],
        dimension_numbers=('NWC', 'WIO', 'NWC'), feature_group_count=u.shape[-1])


def conv_branch(z, conv_w):
    b_gate, c_gate, u = z[..., CB0:CC0], z[..., CC0:CX0], z[..., CX0:G0]
    return b_gate * short_conv(c_gate * u, conv_w)


def merge_branches(z, o_na, o_gqa, y_conv, w_branch, w_o):
    g = jax.nn.sigmoid(z[..., G0:].astype(jnp.float32)).astype(z.dtype)
    g_na, g_gqa, g_conv = jnp.split(g, N_BRANCH, axis=-1)
    m = (g_na * (merge_heads(o_na) @ w_branch[0])
         + g_gqa * (merge_heads(o_gqa) @ w_branch[1])
         + g_conv * (y_conv @ w_branch[2]))
    return m @ w_o


def squared_relu_mlp(h, w_up, w_down):
    return jnp.square(jax.nn.relu(h @ w_up)) @ w_down


def token_mixer(h, hc, w_in, w_branch, w_o, rpb, q_gain, k_gain, conv_w, ang_row, ang_col, with_ctx_out):
    z = h @ w_in
    if with_ctx_out:
        zc = hc @ w_in
        zc_kv = zc[..., KA0:CB0]
    else:
        zc_kv = hc @ w_in[:, KA0:CB0]
    ka_c = split_heads(zc_kv[..., :BRANCH_W], NA_HEADS)
    va_c = split_heads(zc_kv[..., BRANCH_W:2 * BRANCH_W], NA_HEADS)
    kb_c = rms_norm(split_heads(zc_kv[..., 2 * BRANCH_W:2 * BRANCH_W + KV_W], GQA_KV_HEADS), k_gain)
    vb_c = split_heads(zc_kv[..., 2 * BRANCH_W + KV_W:], GQA_KV_HEADS)
    o_na = neighbourhood_attention(
        split_heads(z[..., QA0:QB0], NA_HEADS), split_heads(z[..., KA0:VA0], NA_HEADS),
        split_heads(z[..., VA0:KB0], NA_HEADS), ka_c, va_c, rpb)
    qb = apply_axial_rope(rms_norm(split_heads(z[..., QB0:KA0], GQA_HEADS), q_gain), ang_row, ang_col)
    kb = apply_axial_rope(rms_norm(split_heads(z[..., KB0:VB0], GQA_KV_HEADS), k_gain), ang_row, ang_col)
    o_gqa = blocked_latent_attention(qb, kb, split_heads(z[..., VB0:CB0], GQA_KV_HEADS), kb_c, vb_c)
    out = merge_branches(z, o_na, o_gqa, conv_branch(z, conv_w), w_branch, w_o)
    if not with_ctx_out:
        return out, None
    o_na_c = grouped_attention(split_heads(zc[..., QA0:QB0], NA_HEADS), ka_c, va_c)
    qb_c = rms_norm(split_heads(zc[..., QB0:KA0], GQA_HEADS), q_gain)
    o_gqa_c = grouped_attention(qb_c, kb_c, vb_c)
    out_c = merge_branches(zc, o_na_c, o_gqa_c, conv_branch(zc, conv_w), w_branch, w_o)
    return out, out_c


def setup_inputs(seed: int = 0) -> dict:
    key = jax.random.key(seed)
    ks = jax.random.split(key, 17)

    def nrm(k, shape, s):
        return jax.random.normal(k, shape, jnp.float32) * s

    return {
        'x': nrm(ks[0], (BATCH, SEQ, D_MODEL), 1.0),
        'c': nrm(ks[1], (BATCH, D_MODEL), 1.0),
        'ctx': nrm(ks[2], (BATCH, CTX_LEN, D_MODEL), 1.0),
        'c_ctx': nrm(ks[3], (D_MODEL,), 1.0),
        'w_mod': nrm(ks[4], (DEPTH, D_MODEL, 6 * D_MODEL), 0.5 * D_MODEL ** -0.5),
        'b_mod': nrm(ks[5], (DEPTH, 6 * D_MODEL), 0.02),
        'w_in': nrm(ks[6], (DEPTH, D_MODEL, IN_W), D_MODEL ** -0.5),
        'rpb': nrm(ks[7], (DEPTH, NA_HEADS, 2 * WIN_R - 1, 2 * WIN_C - 1), 0.1),
        'q_gain': 1.0 + nrm(ks[8], (DEPTH, HEAD_DIM), 0.02),
        'k_gain': 1.0 + nrm(ks[9], (DEPTH, HEAD_DIM), 0.02),
        'conv_w': nrm(ks[10], (DEPTH, CONV_K, BRANCH_W), CONV_K ** -0.5),
        'w_branch': nrm(ks[11], (DEPTH, N_BRANCH, BRANCH_W, D_MODEL), DN_BETA * BRANCH_W ** -0.5),
        'w_o': nrm(ks[12], (DEPTH, D_MODEL, D_MODEL), DN_BETA * D_MODEL ** -0.5),
        'w_up': nrm(ks[13], (DEPTH, D_MODEL, MLP_HIDDEN), D_MODEL ** -0.5),
        'w_down': nrm(ks[14], (DEPTH, MLP_HIDDEN, D_MODEL), DN_BETA * MLP_HIDDEN ** -0.5),
        'ln_g': 1.0 + nrm(ks[15], (DEPTH, 2, D_MODEL), 0.02),
        'ln_b': nrm(ks[16], (DEPTH, 2, D_MODEL), 0.02),
    }


def reference(x, c, ctx, c_ctx, w_mod, b_mod, w_in, rpb, q_gain, k_gain, conv_w,
              w_branch, w_o, w_up, w_down, ln_g, ln_b):
    ang_row, ang_col = axial_rope_angles(x.shape[1])
    c_act = jax.nn.silu(c)
    cc_act = jax.nn.silu(c_ctx)
    for l in range(DEPTH):
        with_ctx_out = l < DEPTH - 1
        mod = jnp.split((c_act @ w_mod[l] + b_mod[l])[:, None, :], 6, axis=-1)
        mod_c = jnp.split(cc_act @ w_mod[l] + b_mod[l], 6, axis=-1)
        h = modulate(x, mod[0], mod[1])
        hc = modulate(ctx, mod_c[0], mod_c[1])
        mix, mix_c = token_mixer(h, hc, w_in[l], w_branch[l], w_o[l], rpb[l], q_gain[l], k_gain[l],
                                 conv_w[l], ang_row, ang_col, with_ctx_out)
        x = layer_norm(DN_ALPHA * x + mod[2] * mix) * ln_g[l, 0] + ln_b[l, 0]
        h = modulate(x, mod[3], mod[4])
        x = layer_norm(DN_ALPHA * x + mod[5] * squared_relu_mlp(h, w_up[l], w_down[l])) * ln_g[l, 1] + ln_b[l, 1]
        if with_ctx_out:
            ctx = layer_norm(DN_ALPHA * ctx + mod_c[2] * mix_c) * ln_g[l, 0] + ln_b[l, 0]
            hc = modulate(ctx, mod_c[3], mod_c[4])
            ctx = layer_norm(DN_ALPHA * ctx + mod_c[5] * squared_relu_mlp(hc, w_up[l], w_down[l])) * ln_g[l, 1] + ln_b[l, 1]
    return x
```

```python
import functools

import numpy as np
import jax
import jax.numpy as jnp
from jax import lax
from jax.experimental import pallas as pl
from jax.experimental.pallas import tpu as pltpu

D_MODEL = 2048
GRID_W = 64
HEAD_DIM = 128
BRANCH_W = D_MODEL // 2
NA_HEADS = BRANCH_W // HEAD_DIM
GQA_HEADS = BRANCH_W // HEAD_DIM
GQA_KV_HEADS = GQA_HEADS // 4
GQA_GROUP = GQA_HEADS // GQA_KV_HEADS
KV_W = GQA_KV_HEADS * HEAD_DIM
CONV_K = 3
MLP_HIDDEN = 4 * D_MODEL
N_BRANCH = 3
WIN_R = 8
WIN_C = 16
ROPE_THETA = 10000.0
NORM_EPS = 1e-6
NEG_INF = -1e30
ATTN_SCALE = HEAD_DIM ** -0.5

QA0 = 0
QB0 = QA0 + BRANCH_W
KA0 = QB0 + BRANCH_W
VA0 = KA0 + BRANCH_W
KB0 = VA0 + BRANCH_W
VB0 = KB0 + KV_W
CB0 = VB0 + KV_W
CC0 = CB0 + BRANCH_W
CX0 = CC0 + BRANCH_W
G0 = CX0 + BRANCH_W
IN_W = G0 + N_BRANCH * D_MODEL

MOD_ROWS = 32
NA_Q_ROWS = 4
NA_K_ROWS = 12
NA_QBLK = NA_Q_ROWS * GRID_W
NA_KWIN = NA_K_ROWS * GRID_W

VMEM_LIMIT_BYTES = 56 * 1024 * 1024
LN_CHUNK = 128

F32 = jnp.float32
BF16 = jnp.bfloat16


def _params(*semantics):
    return pltpu.CompilerParams(dimension_semantics=semantics, vmem_limit_bytes=VMEM_LIMIT_BYTES)


def _layer_norm(x):
    mu = jnp.mean(x, axis=-1, keepdims=True)
    xc = x - mu
    var = jnp.mean(xc * xc, axis=-1, keepdims=True)
    return xc * lax.rsqrt(var + NORM_EPS)


def _rms_norm(x, gain):
    return x * lax.rsqrt(jnp.mean(x * x, axis=-1, keepdims=True) + NORM_EPS) * gain


def _rope(x, cos, sin_signed):
    lane = lax.broadcasted_iota(jnp.int32, x.shape, 1)
    partner = jnp.where((lane & 32) == 0, pltpu.roll(x, 96, axis=1), pltpu.roll(x, 32, axis=1))
    return x * cos + partner * sin_signed


def _dot(a, b):
    return jnp.dot(a, b, preferred_element_type=F32)


def _dot_nt(a, b):
    return lax.dot_general(a, b, (((1,), (1,)), ((), ())), preferred_element_type=F32)


def _mod_kernel(c_ref, w_ref, b_ref, o_ref):
    c = c_ref[...]
    a = (c * jax.nn.sigmoid(c)).astype(BF16)
    o_ref[...] = _dot(a, w_ref[...].astype(BF16)) + b_ref[...]


def _modulation(c_all, w_mod, b_mod, tn=1024):
    depth, d, n = w_mod.shape
    return pl.pallas_call(
        _mod_kernel,
        grid=(depth, n // tn),
        in_specs=[
            pl.BlockSpec((MOD_ROWS, d), lambda l, j: (0, 0)),
            pl.BlockSpec((None, d, tn), lambda l, j: (l, 0, j)),
            pl.BlockSpec((None, 1, tn), lambda l, j: (l, 0, j)),
        ],
        out_specs=pl.BlockSpec((None, MOD_ROWS, tn), lambda l, j: (l, 0, j)),
        out_shape=jax.ShapeDtypeStruct((depth, MOD_ROWS, n), F32),
        compiler_params=_params("parallel", "parallel"),
        name="modulation",
    )(c_all, w_mod, b_mod.reshape(depth, 1, n))


def _modulate_into(x_ref, sh_ref, sc_ref, h_ref):
    rows = x_ref.shape[0]
    sc1 = 1.0 + sc_ref[...]
    sh = sh_ref[...]

    def body(c, carry):
        r = pl.multiple_of(c * LN_CHUNK, LN_CHUNK)
        x = x_ref[pl.ds(r, LN_CHUNK), :]
        h_ref[pl.ds(r, LN_CHUNK), :] = (_layer_norm(x) * sc1 + sh).astype(h_ref.dtype)
        return carry

    lax.fori_loop(0, rows // LN_CHUNK, body, 0)


def _ln_matmul_kernel(x_ref, sh_ref, sc_ref, w_ref, o_ref, h_ref):
    @pl.when(pl.program_id(1) == 0)
    def _():
        _modulate_into(x_ref, sh_ref, sc_ref, h_ref)

    o_ref[...] = _dot(h_ref[...], w_ref[...]).astype(o_ref.dtype)


def _mod_row_map(rows_per_mod, tm, fixed_row):
    if fixed_row is not None:
        return lambda i: fixed_row
    return lambda i: (i * tm) // rows_per_mod


def _ln_matmul(x, mod3, shift_blk, scale_blk, w, rows_per_mod, fixed_row, tm, tn):
    m, d = x.shape
    n = w.shape[1]
    row = _mod_row_map(rows_per_mod, tm, fixed_row)
    return pl.pallas_call(
        _ln_matmul_kernel,
        grid=(m // tm, n // tn),
        in_specs=[
            pl.BlockSpec((tm, d), lambda i, j: (i, 0)),
            pl.BlockSpec((None, 1, d), lambda i, j: (row(i), 0, shift_blk)),
            pl.BlockSpec((None, 1, d), lambda i, j: (row(i), 0, scale_blk)),
            pl.BlockSpec((d, tn), lambda i, j: (0, j)),
        ],
        out_specs=pl.BlockSpec((tm, tn), lambda i, j: (i, j)),
        out_shape=jax.ShapeDtypeStruct((m, n), BF16),
        scratch_shapes=[pltpu.VMEM((tm, d), BF16)],
        compiler_params=_params("parallel", "arbitrary"),
        name="ln_matmul",
    )(x, mod3, mod3, w)


def _na_plan(n_rows):
    wr = min(WIN_R, n_rows)
    n_qblk = n_rows // NA_Q_ROWS
    r = np.arange(n_rows)
    row_start = np.clip(r - wr // 2, 0, n_rows - wr)
    starts, pat_ids, patterns, keys = [], [], [], {}
    for qi in range(n_qblk):
        k0 = int(np.clip(qi * NA_Q_ROWS - NA_Q_ROWS, 0, n_rows - NA_K_ROWS))
        q_rows = qi * NA_Q_ROWS + np.arange(NA_Q_ROWS)
        k_rows = k0 + np.arange(NA_K_ROWS)
        valid = (k_rows[None, :] >= row_start[q_rows][:, None]) & (k_rows[None, :] < row_start[q_rows][:, None] + wr)
        assert valid.sum(axis=1).min() == wr, "key window must cover every query row's neighbourhood"
        dr_idx = np.clip(k_rows[None, :] - q_rows[:, None] + WIN_R - 1, 0, 2 * WIN_R - 2)
        key = (valid.tobytes(), dr_idx.tobytes())
        if key not in keys:
            keys[key] = len(patterns)
            patterns.append((valid, dr_idx))
        starts.append(k0)
        pat_ids.append(keys[key])
    return starts, pat_ids, patterns


def _na_bias(rpb, patterns):
    col = np.arange(GRID_W)
    col_start = np.clip(col - WIN_C // 2, 0, GRID_W - WIN_C)
    in_win = (col[None, :] >= col_start[:, None]) & (col[None, :] < col_start[:, None] + WIN_C)
    dc_idx = np.clip(col[None, :] - col[:, None], -(WIN_C - 1), WIN_C - 1) + WIN_C - 1
    out = []
    for valid, dr_idx in patterns:
        b = rpb[:, dr_idx[:, None, :, None], dc_idx[None, :, None, :]].astype(F32)
        mask = valid[:, None, :, None] & in_win[None, :, None, :]
        b = jnp.where(mask[None], b, NEG_INF)
        out.append(b.reshape(rpb.shape[0], NA_QBLK, NA_KWIN))
    return jnp.stack(out, axis=1)


def _na_kernel(q_ref, k_ref, v_ref, kc_ref, vc_ref, bias_ref, o_ref, *, starts, pat_ids):
    kc = kc_ref[...]
    vc = vc_ref[...]
    for qi, (k0, pat) in enumerate(zip(starts, pat_ids)):
        q = q_ref[qi * NA_QBLK:(qi + 1) * NA_QBLK, :]
        kw = k_ref[k0 * GRID_W:k0 * GRID_W + NA_KWIN, :]
        vw = v_ref[k0 * GRID_W:k0 * GRID_W + NA_KWIN, :]
        s_loc = _dot_nt(q, kw) * ATTN_SCALE + bias_ref[pat]
        s_ctx = _dot_nt(q, kc) * ATTN_SCALE
        m = jnp.maximum(jnp.max(s_loc, axis=-1, keepdims=True), jnp.max(s_ctx, axis=-1, keepdims=True))
        p_loc = jnp.exp(s_loc - m)
        p_ctx = jnp.exp(s_ctx - m)
        den = jnp.sum(p_loc, axis=-1, keepdims=True) + jnp.sum(p_ctx, axis=-1, keepdims=True)
        o = _dot(p_loc.astype(BF16), vw) + _dot(p_ctx.astype(BF16), vc)
        o_ref[qi * NA_QBLK:(qi + 1) * NA_QBLK, :] = (o / den).astype(o_ref.dtype)


def _na_attention(z, zc, kc_col, vc_col, bias, plan, batch, seq, ctx_len):
    starts, pat_ids, patterns = plan
    hb = HEAD_DIM
    kernel = functools.partial(_na_kernel, starts=tuple(starts), pat_ids=tuple(pat_ids))
    return pl.pallas_call(
        kernel,
        grid=(NA_HEADS, batch),
        in_specs=[
            pl.BlockSpec((seq, hb), lambda h, b: (b, QA0 // hb + h)),
            pl.BlockSpec((seq, hb), lambda h, b: (b, KA0 // hb + h)),
            pl.BlockSpec((seq, hb), lambda h, b: (b, VA0 // hb + h)),
            pl.BlockSpec((ctx_len, hb), lambda h, b: (b, kc_col // hb + h)),
            pl.BlockSpec((ctx_len, hb), lambda h, b: (b, vc_col // hb + h)),
            pl.BlockSpec((None, len(patterns), NA_QBLK, NA_KWIN), lambda h, b: (h, 0, 0, 0)),
        ],
        out_specs=pl.BlockSpec((seq, hb), lambda h, b: (b, h)),
        out_shape=jax.ShapeDtypeStruct((batch * seq, BRANCH_W), BF16),
        compiler_params=_params("parallel", "parallel"),
        name="na_attention",
    )(z, z, z, zc, zc, bias)


def _gqa_kernel(q_ref, k_ref, v_ref, kc_ref, vc_ref, qg_ref, kg_ref, cos_ref, sin_ref, o_ref,
                ks_ref, vs_ref, *, tq, seq):
    qi = pl.program_id(2)

    @pl.when(qi == 0)
    def _():
        kg = kg_ref[...]
        kn = _rms_norm(k_ref[...].astype(F32), kg)
        ks_ref[0:seq, :] = _rope(kn, cos_ref[...], sin_ref[...]).astype(BF16)
        ks_ref[seq:, :] = _rms_norm(kc_ref[...].astype(F32), kg).astype(BF16)
        vs_ref[0:seq, :] = v_ref[...]
        vs_ref[seq:, :] = vc_ref[...]

    r = pl.multiple_of(qi * tq, tq)
    cos = cos_ref[pl.ds(r, tq), :]
    sin = sin_ref[pl.ds(r, tq), :]
    qg = qg_ref[...]
    for g in range(GQA_GROUP):
        q = q_ref[:, g * HEAD_DIM:(g + 1) * HEAD_DIM].astype(F32)
        qn = _rope(_rms_norm(q, qg), cos, sin).astype(BF16)
        s = _dot_nt(qn, ks_ref[...]) * ATTN_SCALE
        m = jnp.max(s, axis=-1, keepdims=True)
        p = jnp.exp(s - m)
        den = jnp.sum(p, axis=-1, keepdims=True)
        o = _dot(p.astype(BF16), vs_ref[...])
        o_ref[:, g * HEAD_DIM:(g + 1) * HEAD_DIM] = (o / den).astype(o_ref.dtype)


def _gqa_attention(z, zc, kc_col, vc_col, q_gain, k_gain, cos, sin, batch, seq, ctx_len, tq=256):
    hb = HEAD_DIM
    gw = GQA_GROUP * HEAD_DIM
    nq = seq // tq
    kernel = functools.partial(_gqa_kernel, tq=tq, seq=seq)
    return pl.pallas_call(
        kernel,
        grid=(batch, GQA_KV_HEADS, nq),
        in_specs=[
            pl.BlockSpec((tq, gw), lambda b, kh, qi: (b * nq + qi, QB0 // gw + kh)),
            pl.BlockSpec((seq, hb), lambda b, kh, qi: (b, KB0 // hb + kh)),
            pl.BlockSpec((seq, hb), lambda b, kh, qi: (b, VB0 // hb + kh)),
            pl.BlockSpec((ctx_len, hb), lambda b, kh, qi: (b, kc_col // hb + kh)),
            pl.BlockSpec((ctx_len, hb), lambda b, kh, qi: (b, vc_col // hb + kh)),
            pl.BlockSpec((1, hb), lambda b, kh, qi: (0, 0)),
            pl.BlockSpec((1, hb), lambda b, kh, qi: (0, 0)),
            pl.BlockSpec((seq, hb), lambda b, kh, qi: (0, 0)),
            pl.BlockSpec((seq, hb), lambda b, kh, qi: (0, 0)),
        ],
        out_specs=pl.BlockSpec((tq, gw), lambda b, kh, qi: (b * nq + qi, kh)),
        out_shape=jax.ShapeDtypeStruct((batch * seq, BRANCH_W), BF16),
        scratch_shapes=[pltpu.VMEM((seq + ctx_len, hb), BF16), pltpu.VMEM((seq + ctx_len, hb), BF16)],
        compiler_params=_params("parallel", "parallel", "arbitrary"),
        name="gqa_attention",
    )(z, z, z, zc, zc, q_gain, k_gain, cos, sin)


def _ctx_attn_kernel(q_ref, k_ref, v_ref, qg_ref, kg_ref, o_ref, *, qk_norm):
    q = q_ref[...]
    k = k_ref[...]
    if qk_norm:
        q = _rms_norm(q.astype(F32), qg_ref[...]).astype(BF16)
        k = _rms_norm(k.astype(F32), kg_ref[...]).astype(BF16)
    s = _dot_nt(q, k) * ATTN_SCALE
    m = jnp.max(s, axis=-1, keepdims=True)
    p = jnp.exp(s - m)
    den = jnp.sum(p, axis=-1, keepdims=True)
    o_ref[...] = (_dot(p.astype(BF16), v_ref[...]) / den).astype(o_ref.dtype)


def _ctx_attention(zc, q_col, k_col, v_col, kv_group, q_gain, k_gain, qk_norm, batch, ctx_len):
    hb = HEAD_DIM
    kernel = functools.partial(_ctx_attn_kernel, qk_norm=qk_norm)
    return pl.pallas_call(
        kernel,
        grid=(batch, BRANCH_W // hb),
        in_specs=[
            pl.BlockSpec((ctx_len, hb), lambda b, h: (b, q_col // hb + h)),
            pl.BlockSpec((ctx_len, hb), lambda b, h: (b, k_col // hb + h // kv_group)),
            pl.BlockSpec((ctx_len, hb), lambda b, h: (b, v_col // hb + h // kv_group)),
            pl.BlockSpec((1, hb), lambda b, h: (0, 0)),
            pl.BlockSpec((1, hb), lambda b, h: (0, 0)),
        ],
        out_specs=pl.BlockSpec((ctx_len, hb), lambda b, h: (b, h)),
        out_shape=jax.ShapeDtypeStruct((batch * ctx_len, BRANCH_W), BF16),
        compiler_params=_params("parallel", "parallel"),
        name="ctx_attention",
    )(zc, zc, zc, q_gain, k_gain)


def _conv_kernel(b_ref, c_ref, u_ref, w_ref, o_ref):
    v = c_ref[...].astype(F32) * u_ref[...].astype(F32)
    n = v.shape[0]
    row = lax.broadcasted_iota(jnp.int32, v.shape, 0)
    prev = jnp.where(row == 0, 0.0, pltpu.roll(v, 1, axis=0))
    nxt = jnp.where(row == n - 1, 0.0, pltpu.roll(v, n - 1, axis=0))
    w = w_ref[...]
    y = w[0:1, :] * prev + w[1:2, :] * v + w[2:3, :] * nxt
    o_ref[...] = (b_ref[...].astype(F32) * y).astype(o_ref.dtype)


def _short_conv(z, conv_w, n_seq, seq_len, tc=256):
    return pl.pallas_call(
        _conv_kernel,
        grid=(n_seq, BRANCH_W // tc),
        in_specs=[
            pl.BlockSpec((seq_len, tc), lambda s, j: (s, CB0 // tc + j)),
            pl.BlockSpec((seq_len, tc), lambda s, j: (s, CC0 // tc + j)),
            pl.BlockSpec((seq_len, tc), lambda s, j: (s, CX0 // tc + j)),
            pl.BlockSpec((CONV_K, tc), lambda s, j: (0, j)),
        ],
        out_specs=pl.BlockSpec((seq_len, tc), lambda s, j: (s, j)),
        out_shape=jax.ShapeDtypeStruct((n_seq * seq_len, BRANCH_W), BF16),
        compiler_params=_params("parallel", "parallel"),
        name="short_conv",
    )(z, z, z, conv_w)


def _merge_kernel(a0_ref, a1_ref, a2_ref, g0_ref, g1_ref, g2_ref, w_ref, o_ref):
    acc = None
    for k, (a_ref, g_ref) in enumerate(((a0_ref, g0_ref), (a1_ref, g1_ref), (a2_ref, g2_ref))):
        term = jax.nn.sigmoid(g_ref[...].astype(F32)) * _dot(a_ref[...], w_ref[k])
        acc = term if acc is None else acc + term
    o_ref[...] = acc.astype(o_ref.dtype)


def _merge(o_na, o_gqa, y_conv, z, w_branch, tm=512, tn=512):
    m = o_na.shape[0]
    assert G0 % tn == 0 and D_MODEL % tn == 0
    gate_blk = lambda k: (lambda j, i: (i, (G0 + k * D_MODEL) // tn + j))
    act = pl.BlockSpec((tm, BRANCH_W), lambda j, i: (i, 0))
    return pl.pallas_call(
        _merge_kernel,
        grid=(D_MODEL // tn, m // tm),
        in_specs=[act, act, act,
                  pl.BlockSpec((tm, tn), gate_blk(0)),
                  pl.BlockSpec((tm, tn), gate_blk(1)),
                  pl.BlockSpec((tm, tn), gate_blk(2)),
                  pl.BlockSpec((N_BRANCH, BRANCH_W, tn), lambda j, i: (0, 0, j))],
        out_specs=pl.BlockSpec((tm, tn), lambda j, i: (i, j)),
        out_shape=jax.ShapeDtypeStruct((m, D_MODEL), BF16),
        compiler_params=_params("parallel", "parallel"),
        name="merge",
    )(o_na, o_gqa, y_conv, z, z, z, w_branch)


def _residual_norm_inplace(x_ref, gate_ref, g_ref, b_ref, o_ref, alpha):
    gate = gate_ref[...]
    g = g_ref[...]
    b = b_ref[...]

    def body(c, carry):
        r = pl.multiple_of(c * LN_CHUNK, LN_CHUNK)
        y = alpha * x_ref[pl.ds(r, LN_CHUNK), :] + gate * o_ref[pl.ds(r, LN_CHUNK), :]
        o_ref[pl.ds(r, LN_CHUNK), :] = _layer_norm(y) * g + b
        return carry

    lax.fori_loop(0, x_ref.shape[0] // LN_CHUNK, body, 0)


def _out_proj_kernel(m_ref, w_ref, x_ref, gate_ref, g_ref, b_ref, o_ref, *, alpha):
    o_ref[...] = _dot(m_ref[...], w_ref[...])
    _residual_norm_inplace(x_ref, gate_ref, g_ref, b_ref, o_ref, alpha)


def _out_proj(m_act, w_o, x, mod3, gate_blk, ln_g, ln_b, alpha, rows_per_mod, fixed_row, tm=512):
    m, d = x.shape
    row = _mod_row_map(rows_per_mod, tm, fixed_row)
    vec = pl.BlockSpec((1, d), lambda i: (0, 0))
    return pl.pallas_call(
        functools.partial(_out_proj_kernel, alpha=alpha),
        grid=(m // tm,),
        in_specs=[
            pl.BlockSpec((tm, d), lambda i: (i, 0)),
            pl.BlockSpec((d, d), lambda i: (0, 0)),
            pl.BlockSpec((tm, d), lambda i: (i, 0)),
            pl.BlockSpec((None, 1, d), lambda i: (row(i), 0, gate_blk)),
            vec, vec,
        ],
        out_specs=pl.BlockSpec((tm, d), lambda i: (i, 0)),
        out_shape=jax.ShapeDtypeStruct((m, d), F32),
        compiler_params=_params("parallel"),
        name="out_proj",
    )(m_act, w_o, x, mod3, ln_g, ln_b)


def _mlp_kernel(x_ref, sh_ref, sc_ref, gate_ref, g_ref, b_ref, wu_ref, wd_ref, o_ref, h_ref, *, alpha):
    j = pl.program_id(1)

    @pl.when(j == 0)
    def _():
        _modulate_into(x_ref, sh_ref, sc_ref, h_ref)

    u = jnp.maximum(_dot(h_ref[...], wu_ref[...]), 0.0)
    part = _dot((u * u).astype(BF16), wd_ref[...])

    @pl.when(j == 0)
    def _():
        o_ref[...] = part

    @pl.when(j > 0)
    def _():
        o_ref[...] += part

    @pl.when(j == pl.num_programs(1) - 1)
    def _():
        _residual_norm_inplace(x_ref, gate_ref, g_ref, b_ref, o_ref, alpha)


def _mlp(x, mod3, shift_blk, scale_blk, gate_blk, ln_g, ln_b, w_up, w_down, alpha, rows_per_mod, fixed_row,
         tm=512, th=1024):
    m, d = x.shape
    hid = w_up.shape[1]
    row = _mod_row_map(rows_per_mod, tm, fixed_row)
    mod_spec = lambda blk: pl.BlockSpec((None, 1, d), lambda i, j: (row(i), 0, blk))
    vec = pl.BlockSpec((1, d), lambda i, j: (0, 0))
    return pl.pallas_call(
        functools.partial(_mlp_kernel, alpha=alpha),
        grid=(m // tm, hid // th),
        in_specs=[
            pl.BlockSpec((tm, d), lambda i, j: (i, 0)),
            mod_spec(shift_blk), mod_spec(scale_blk), mod_spec(gate_blk),
            vec, vec,
            pl.BlockSpec((d, th), lambda i, j: (0, j)),
            pl.BlockSpec((th, d), lambda i, j: (j, 0)),
        ],
        out_specs=pl.BlockSpec((tm, d), lambda i, j: (i, 0)),
        out_shape=jax.ShapeDtypeStruct((m, d), F32),
        scratch_shapes=[pltpu.VMEM((tm, d), BF16)],
        compiler_params=_params("parallel", "arbitrary"),
        name="mlp",
    )(x, mod3, mod3, mod3, ln_g, ln_b, w_up, w_down)


def _rope_tables(seq):
    t = jnp.arange(seq)
    row = (t // GRID_W).astype(F32)
    col = (t % GRID_W).astype(F32)
    axis_dim = HEAD_DIM // 2
    freqs = ROPE_THETA ** (-jnp.arange(0, axis_dim, 2, dtype=F32) / axis_dim)
    ar, ac = row[:, None] * freqs, col[:, None] * freqs
    cos = jnp.concatenate([jnp.cos(ar), jnp.cos(ar), jnp.cos(ac), jnp.cos(ac)], axis=-1)
    sin = jnp.concatenate([-jnp.sin(ar), jnp.sin(ar), -jnp.sin(ac), jnp.sin(ac)], axis=-1)
    return cos, sin


def kernel(x, c, ctx, c_ctx, w_mod, b_mod, w_in, rpb, q_gain, k_gain, conv_w, w_branch, w_o, w_up, w_down,
           ln_g, ln_b):
    batch, seq, d = x.shape
    ctx_len = ctx.shape[1]
    depth = w_mod.shape[0]
    assert d == D_MODEL and batch < MOD_ROWS and seq % NA_QBLK == 0 and seq // GRID_W >= NA_K_ROWS
    alpha = float((2 * depth) ** 0.25)
    ctx_row = batch

    c_all = jnp.concatenate([c, c_ctx[None, :], jnp.zeros((MOD_ROWS - batch - 1, d), F32)], axis=0)
    mod = _modulation(c_all, w_mod, b_mod)
    cos, sin = _rope_tables(seq)
    plan = _na_plan(seq // GRID_W)

    xl = x.reshape(batch * seq, d)
    xc = ctx.reshape(batch * ctx_len, d)
    for l in range(depth):
        with_ctx_out = l < depth - 1
        mod3 = mod[l].reshape(MOD_ROWS, 1, 6 * d)
        w_in_l = w_in[l].astype(BF16)
        wb_l = w_branch[l].astype(BF16)
        wo_l = w_o[l].astype(BF16)
        wu_l = w_up[l].astype(BF16)
        wd_l = w_down[l].astype(BF16)
        qg, kg = q_gain[l][None, :], k_gain[l][None, :]
        g1, b1 = ln_g[l, 0][None, :], ln_b[l, 0][None, :]
        g2, b2 = ln_g[l, 1][None, :], ln_b[l, 1][None, :]
        bias = _na_bias(rpb[l], plan[2])

        z = _ln_matmul(xl, mod3, 0, 1, w_in_l, seq, None, tm=512, tn=1536)
        if with_ctx_out:
            zc = _ln_matmul(xc, mod3, 0, 1, w_in_l, None, ctx_row, tm=512, tn=1536)
            ka_c, va_c, kb_c, vb_c = KA0, VA0, KB0, VB0
        else:
            zc = _ln_matmul(xc, mod3, 0, 1, w_in_l[:, KA0:CB0], None, ctx_row, tm=512, tn=(CB0 - KA0) // 2)
            ka_c, va_c, kb_c, vb_c = 0, BRANCH_W, 2 * BRANCH_W, 2 * BRANCH_W + KV_W

        o_na = _na_attention(z, zc, ka_c, va_c, bias, plan, batch, seq, ctx_len)
        o_gqa = _gqa_attention(z, zc, kb_c, vb_c, qg, kg, cos, sin, batch, seq, ctx_len)
        y_conv = _short_conv(z, conv_w[l], batch, seq)
        m_act = _merge(o_na, o_gqa, y_conv, z, wb_l)
        xl = _out_proj(m_act, wo_l, xl, mod3, 2, g1, b1, alpha, seq, None)
        xl = _mlp(xl, mod3, 3, 4, 5, g2, b2, wu_l, wd_l, alpha, seq, None)

        if with_ctx_out:
            o_na_c = _ctx_attention(zc, QA0, KA0, VA0, 1, qg, kg, False, batch, ctx_len)
            o_gqa_c = _ctx_attention(zc, QB0, KB0, VB0, GQA_GROUP, qg, kg, True, batch, ctx_len)
            y_conv_c = _short_conv(zc, conv_w[l], batch, ctx_len)
            m_c = _merge(o_na_c, o_gqa_c, y_conv_c, zc, wb_l)
            xc = _out_proj(m_c, wo_l, xc, mod3, 2, g1, b1, alpha, None, ctx_row)
            xc = _mlp(xc, mod3, 3, 4, 5, g2, b2, wu_l, wd_l, alpha, None, ctx_row)
    return xl.reshape(batch, seq, d)
```

```python
import functools

import numpy as np
import jax
import jax.numpy as jnp
from jax import lax
from jax.experimental import pallas as pl
from jax.experimental.pallas import tpu as pltpu

D_MODEL = 2048
GRID_W = 64
HEAD_DIM = 128
BRANCH_W = D_MODEL // 2
NA_HEADS = BRANCH_W // HEAD_DIM
GQA_HEADS = BRANCH_W // HEAD_DIM
GQA_KV_HEADS = GQA_HEADS // 4
GQA_GROUP = GQA_HEADS // GQA_KV_HEADS
KV_W = GQA_KV_HEADS * HEAD_DIM
CONV_K = 3
MLP_HIDDEN = 4 * D_MODEL
N_BRANCH = 3
WIN_R = 8
WIN_C = 16
ROPE_THETA = 10000.0
NORM_EPS = 1e-6
NEG_INF = -1e30
ATTN_SCALE = HEAD_DIM ** -0.5
LOG2E = 1.4426950408889634
SCORE_SCALE = ATTN_SCALE * LOG2E

QA0 = 0
QB0 = QA0 + BRANCH_W
KA0 = QB0 + BRANCH_W
VA0 = KA0 + BRANCH_W
KB0 = VA0 + BRANCH_W
VB0 = KB0 + KV_W
CB0 = VB0 + KV_W
CC0 = CB0 + BRANCH_W
CX0 = CC0 + BRANCH_W
G0 = CX0 + BRANCH_W
IN_W = G0 + N_BRANCH * D_MODEL

MOD_ROWS = 32
NA_Q_ROWS = 4
NA_K_ROWS = 12
NA_QBLK = NA_Q_ROWS * GRID_W
NA_KWIN = NA_K_ROWS * GRID_W

VMEM_LIMIT_BYTES = 56 * 1024 * 1024
LN_CHUNK = 128

F32 = jnp.float32
BF16 = jnp.bfloat16


def _params(*semantics):
    return pltpu.CompilerParams(dimension_semantics=semantics, vmem_limit_bytes=VMEM_LIMIT_BYTES)


def _layer_norm(x):
    mu = jnp.mean(x, axis=-1, keepdims=True)
    xc = x - mu
    var = jnp.mean(xc * xc, axis=-1, keepdims=True)
    return xc * lax.rsqrt(var + NORM_EPS)


def _rms_norm(x, gain):
    return x * lax.rsqrt(jnp.mean(x * x, axis=-1, keepdims=True) + NORM_EPS) * gain


def _rope(x, cos, sin_signed):
    lane = lax.broadcasted_iota(jnp.int32, x.shape, 1)
    partner = jnp.where((lane & 32) == 0, pltpu.roll(x, 96, axis=1), pltpu.roll(x, 32, axis=1))
    return x * cos + partner * sin_signed


def _dot(a, b):
    return jnp.dot(a, b, preferred_element_type=F32)


def _dot_nt(a, b):
    return lax.dot_general(a, b, (((1,), (1,)), ((), ())), preferred_element_type=F32)


def _mod_kernel(c_ref, w_ref, b_ref, o_ref):
    c = c_ref[...]
    a = (c * jax.nn.sigmoid(c)).astype(BF16)
    o_ref[...] = _dot(a, w_ref[...].astype(BF16)) + b_ref[...]


def _modulation(c_all, w_mod, b_mod, tn=1024):
    depth, d, n = w_mod.shape
    return pl.pallas_call(
        _mod_kernel,
        grid=(depth, n // tn),
        in_specs=[
            pl.BlockSpec((MOD_ROWS, d), lambda l, j: (0, 0)),
            pl.BlockSpec((None, d, tn), lambda l, j: (l, 0, j)),
            pl.BlockSpec((None, 1, tn), lambda l, j: (l, 0, j)),
        ],
        out_specs=pl.BlockSpec((None, MOD_ROWS, tn), lambda l, j: (l, 0, j)),
        out_shape=jax.ShapeDtypeStruct((depth, MOD_ROWS, n), F32),
        compiler_params=_params("parallel", "parallel"),
        name="modulation",
    )(c_all, w_mod, b_mod.reshape(depth, 1, n))


def _modulate_into(x_ref, sh_ref, sc_ref, h_ref):
    rows = x_ref.shape[0]
    sc1 = 1.0 + sc_ref[...]
    sh = sh_ref[...]

    def body(c, carry):
        r = pl.multiple_of(c * LN_CHUNK, LN_CHUNK)
        x = x_ref[pl.ds(r, LN_CHUNK), :]
        h_ref[pl.ds(r, LN_CHUNK), :] = (_layer_norm(x) * sc1 + sh).astype(h_ref.dtype)
        return carry

    lax.fori_loop(0, rows // LN_CHUNK, body, 0)


def _ln_matmul_kernel(x_ref, sh_ref, sc_ref, w_ref, o_ref, h_ref):
    @pl.when(pl.program_id(1) == 0)
    def _():
        _modulate_into(x_ref, sh_ref, sc_ref, h_ref)

    o_ref[...] = _dot(h_ref[...], w_ref[...]).astype(o_ref.dtype)


def _mod_row_map(rows_per_mod, tm, fixed_row):
    if fixed_row is not None:
        return lambda i: fixed_row
    return lambda i: (i * tm) // rows_per_mod


def _ln_matmul(x, mod3, shift_blk, scale_blk, w, rows_per_mod, fixed_row, tm, tn):
    m, d = x.shape
    n = w.shape[1]
    row = _mod_row_map(rows_per_mod, tm, fixed_row)
    return pl.pallas_call(
        _ln_matmul_kernel,
        grid=(m // tm, n // tn),
        in_specs=[
            pl.BlockSpec((tm, d), lambda i, j: (i, 0)),
            pl.BlockSpec((None, 1, d), lambda i, j: (row(i), 0, shift_blk)),
            pl.BlockSpec((None, 1, d), lambda i, j: (row(i), 0, scale_blk)),
            pl.BlockSpec((d, tn), lambda i, j: (0, j)),
        ],
        out_specs=pl.BlockSpec((tm, tn), lambda i, j: (i, j)),
        out_shape=jax.ShapeDtypeStruct((m, n), BF16),
        scratch_shapes=[pltpu.VMEM((tm, d), BF16)],
        compiler_params=_params("parallel", "arbitrary"),
        name="ln_matmul",
    )(x, mod3, mod3, w)


def _na_plan(n_rows):
    wr = min(WIN_R, n_rows)
    n_qblk = n_rows // NA_Q_ROWS
    r = np.arange(n_rows)
    row_start = np.clip(r - wr // 2, 0, n_rows - wr)
    starts, pat_ids, patterns, keys = [], [], [], {}
    for qi in range(n_qblk):
        k0 = int(np.clip(qi * NA_Q_ROWS - NA_Q_ROWS, 0, n_rows - NA_K_ROWS))
        q_rows = qi * NA_Q_ROWS + np.arange(NA_Q_ROWS)
        k_rows = k0 + np.arange(NA_K_ROWS)
        valid = (k_rows[None, :] >= row_start[q_rows][:, None]) & (k_rows[None, :] < row_start[q_rows][:, None] + wr)
        assert valid.sum(axis=1).min() == wr, "key window must cover every query row's neighbourhood"
        dr_idx = np.clip(k_rows[None, :] - q_rows[:, None] + WIN_R - 1, 0, 2 * WIN_R - 2)
        key = (valid.tobytes(), dr_idx.tobytes())
        if key not in keys:
            keys[key] = len(patterns)
            patterns.append((valid, dr_idx))
        starts.append(k0)
        pat_ids.append(keys[key])
    return starts, pat_ids, patterns


def _na_bias(rpb, patterns):
    heads, n_dr, n_dc = rpb.shape
    col = np.arange(GRID_W)
    col_start = np.clip(col - WIN_C // 2, 0, GRID_W - WIN_C)
    in_win = (col[None, :] >= col_start[:, None]) & (col[None, :] < col_start[:, None] + WIN_C)
    assert np.abs(col[None, :] - col[:, None])[in_win].max() <= WIN_C - 1
    line = 2 * GRID_W
    left = GRID_W - 1 - (WIN_C - 1)
    w = jnp.pad(rpb.astype(F32), ((0, 0), (0, 0), (left, line - left - n_dc)))
    w = jnp.broadcast_to(w[:, :, None, :], (heads, n_dr, GRID_W, line)).reshape(heads, n_dr, GRID_W * line)
    toe = w[:, :, :GRID_W * (line - 1)].reshape(heads, n_dr, GRID_W, line - 1)[..., GRID_W - 1:]
    toe = jnp.where(in_win[None, None], toe * LOG2E, NEG_INF)
    out = []
    for valid, dr_idx in patterns:
        blocks = jnp.stack([toe[:, int(i)] for i in dr_idx.reshape(-1)], axis=1)
        blocks = blocks.reshape(heads, NA_Q_ROWS, NA_K_ROWS, GRID_W, GRID_W)
        blocks = jnp.where(valid[None, :, :, None, None], blocks, NEG_INF)
        out.append(blocks.transpose(0, 1, 3, 2, 4).reshape(heads, NA_QBLK, NA_KWIN))
    return jnp.stack(out, axis=1)


def _fill_values_with_ones(vs_ref, v_ref, vc_ref, seq):
    vs_ref[0:seq, 0:HEAD_DIM] = v_ref[...]
    vs_ref[seq:, 0:HEAD_DIM] = vc_ref[...]
    vs_ref[:, HEAD_DIM:] = jnp.ones((vs_ref.shape[0], HEAD_DIM), vs_ref.dtype)


def _normalised(o_aug):
    return o_aug[:, :HEAD_DIM] / o_aug[:, HEAD_DIM:]


def _na_kernel(q_ref, k_ref, v_ref, kc_ref, vc_ref, bias_ref, o_ref, vs_ref, *, starts, pat_ids, seq):
    _fill_values_with_ones(vs_ref, v_ref, vc_ref, seq)
    kc = kc_ref[...]
    for qi, (k0, pat) in enumerate(zip(starts, pat_ids)):
        q = q_ref[qi * NA_QBLK:(qi + 1) * NA_QBLK, :]
        kw = k_ref[k0 * GRID_W:k0 * GRID_W + NA_KWIN, :]
        s_loc = _dot_nt(q, kw) * SCORE_SCALE + bias_ref[pat]
        s_ctx = _dot_nt(q, kc) * SCORE_SCALE
        m = jnp.maximum(jnp.max(s_loc, axis=-1, keepdims=True), jnp.max(s_ctx, axis=-1, keepdims=True))
        p_loc = jnp.exp2(s_loc - m).astype(BF16)
        p_ctx = jnp.exp2(s_ctx - m).astype(BF16)
        o_aug = _dot(p_loc, vs_ref[k0 * GRID_W:k0 * GRID_W + NA_KWIN, :]) + _dot(p_ctx, vs_ref[seq:, :])
        o_ref[qi * NA_QBLK:(qi + 1) * NA_QBLK, :] = _normalised(o_aug).astype(o_ref.dtype)


def _na_attention(z, zc, kc_col, vc_col, bias, plan, batch, seq, ctx_len):
    starts, pat_ids, patterns = plan
    hb = HEAD_DIM
    kernel = functools.partial(_na_kernel, starts=tuple(starts), pat_ids=tuple(pat_ids), seq=seq)
    return pl.pallas_call(
        kernel,
        grid=(NA_HEADS, batch),
        in_specs=[
            pl.BlockSpec((seq, hb), lambda h, b: (b, QA0 // hb + h)),
            pl.BlockSpec((seq, hb), lambda h, b: (b, KA0 // hb + h)),
            pl.BlockSpec((seq, hb), lambda h, b: (b, VA0 // hb + h)),
            pl.BlockSpec((ctx_len, hb), lambda h, b: (b, kc_col // hb + h)),
            pl.BlockSpec((ctx_len, hb), lambda h, b: (b, vc_col // hb + h)),
            pl.BlockSpec((None, len(patterns), NA_QBLK, NA_KWIN), lambda h, b: (h, 0, 0, 0)),
        ],
        out_specs=pl.BlockSpec((seq, hb), lambda h, b: (b, h)),
        out_shape=jax.ShapeDtypeStruct((batch * seq, BRANCH_W), BF16),
        scratch_shapes=[pltpu.VMEM((seq + ctx_len, 2 * hb), BF16)],
        compiler_params=_params("parallel", "parallel"),
        name="na_attention",
    )(z, z, z, zc, zc, bias)


def _gqa_kernel(q_ref, k_ref, v_ref, kc_ref, vc_ref, qg_ref, kg_ref, cos_ref, sin_ref, o_ref,
                ks_ref, vs_ref, *, tq, seq):
    qi = pl.program_id(2)

    @pl.when(qi == 0)
    def _():
        kg = kg_ref[...]
        kn = _rms_norm(k_ref[...].astype(F32), kg)
        ks_ref[0:seq, :] = _rope(kn, cos_ref[...], sin_ref[...]).astype(BF16)
        ks_ref[seq:, :] = _rms_norm(kc_ref[...].astype(F32), kg).astype(BF16)
        _fill_values_with_ones(vs_ref, v_ref, vc_ref, seq)

    r = pl.multiple_of(qi * tq, tq)
    cos = cos_ref[pl.ds(r, tq), :]
    sin = sin_ref[pl.ds(r, tq), :]
    qg = qg_ref[...]
    for g in range(GQA_GROUP):
        q = q_ref[:, g * HEAD_DIM:(g + 1) * HEAD_DIM].astype(F32)
        qn = (_rope(_rms_norm(q, qg), cos, sin) * SCORE_SCALE).astype(BF16)
        s = _dot_nt(qn, ks_ref[...])
        m = jnp.max(s, axis=-1, keepdims=True)
        p = jnp.exp2(s - m).astype(BF16)
        o_aug = _dot(p, vs_ref[...])
        o_ref[:, g * HEAD_DIM:(g + 1) * HEAD_DIM] = _normalised(o_aug).astype(o_ref.dtype)


def _gqa_attention(z, zc, kc_col, vc_col, q_gain, k_gain, cos, sin, batch, seq, ctx_len, tq=256):
    hb = HEAD_DIM
    gw = GQA_GROUP * HEAD_DIM
    nq = seq // tq
    kernel = functools.partial(_gqa_kernel, tq=tq, seq=seq)
    return pl.pallas_call(
        kernel,
        grid=(batch, GQA_KV_HEADS, nq),
        in_specs=[
            pl.BlockSpec((tq, gw), lambda b, kh, qi: (b * nq + qi, QB0 // gw + kh)),
            pl.BlockSpec((seq, hb), lambda b, kh, qi: (b, KB0 // hb + kh)),
            pl.BlockSpec((seq, hb), lambda b, kh, qi: (b, VB0 // hb + kh)),
            pl.BlockSpec((ctx_len, hb), lambda b, kh, qi: (b, kc_col // hb + kh)),
            pl.BlockSpec((ctx_len, hb), lambda b, kh, qi: (b, vc_col // hb + kh)),
            pl.BlockSpec((1, hb), lambda b, kh, qi: (0, 0)),
            pl.BlockSpec((1, hb), lambda b, kh, qi: (0, 0)),
            pl.BlockSpec((seq, hb), lambda b, kh, qi: (0, 0)),
            pl.BlockSpec((seq, hb), lambda b, kh, qi: (0, 0)),
        ],
        out_specs=pl.BlockSpec((tq, gw), lambda b, kh, qi: (b * nq + qi, kh)),
        out_shape=jax.ShapeDtypeStruct((batch * seq, BRANCH_W), BF16),
        scratch_shapes=[pltpu.VMEM((seq + ctx_len, hb), BF16), pltpu.VMEM((seq + ctx_len, 2 * hb), BF16)],
        compiler_params=_params("parallel", "parallel", "arbitrary"),
        name="gqa_attention",
    )(z, z, z, zc, zc, q_gain, k_gain, cos, sin)


def _ctx_attn_kernel(q_ref, k_ref, v_ref, qg_ref, kg_ref, o_ref, *, qk_norm):
    q = q_ref[...]
    k = k_ref[...]
    if qk_norm:
        q = _rms_norm(q.astype(F32), qg_ref[...]).astype(BF16)
        k = _rms_norm(k.astype(F32), kg_ref[...]).astype(BF16)
    s = _dot_nt(q, k) * ATTN_SCALE
    m = jnp.max(s, axis=-1, keepdims=True)
    p = jnp.exp(s - m)
    den = jnp.sum(p, axis=-1, keepdims=True)
    o_ref[...] = (_dot(p.astype(BF16), v_ref[...]) / den).astype(o_ref.dtype)


def _ctx_attention(zc, q_col, k_col, v_col, kv_group, q_gain, k_gain, qk_norm, batch, ctx_len):
    hb = HEAD_DIM
    kernel = functools.partial(_ctx_attn_kernel, qk_norm=qk_norm)
    return pl.pallas_call(
        kernel,
        grid=(batch, BRANCH_W // hb),
        in_specs=[
            pl.BlockSpec((ctx_len, hb), lambda b, h: (b, q_col // hb + h)),
            pl.BlockSpec((ctx_len, hb), lambda b, h: (b, k_col // hb + h // kv_group)),
            pl.BlockSpec((ctx_len, hb), lambda b, h: (b, v_col // hb + h // kv_group)),
            pl.BlockSpec((1, hb), lambda b, h: (0, 0)),
            pl.BlockSpec((1, hb), lambda b, h: (0, 0)),
        ],
        out_specs=pl.BlockSpec((ctx_len, hb), lambda b, h: (b, h)),
        out_shape=jax.ShapeDtypeStruct((batch * ctx_len, BRANCH_W), BF16),
        compiler_params=_params("parallel", "parallel"),
        name="ctx_attention",
    )(zc, zc, zc, q_gain, k_gain)


def _conv_kernel(b_ref, c_ref, u_ref, w_ref, o_ref):
    v = c_ref[...].astype(F32) * u_ref[...].astype(F32)
    n = v.shape[0]
    row = lax.broadcasted_iota(jnp.int32, v.shape, 0)
    prev = jnp.where(row == 0, 0.0, pltpu.roll(v, 1, axis=0))
    nxt = jnp.where(row == n - 1, 0.0, pltpu.roll(v, n - 1, axis=0))
    w = w_ref[...]
    y = w[0:1, :] * prev + w[1:2, :] * v + w[2:3, :] * nxt
    o_ref[...] = (b_ref[...].astype(F32) * y).astype(o_ref.dtype)


def _short_conv(z, conv_w, n_seq, seq_len, tc=256):
    return pl.pallas_call(
        _conv_kernel,
        grid=(n_seq, BRANCH_W // tc),
        in_specs=[
            pl.BlockSpec((seq_len, tc), lambda s, j: (s, CB0 // tc + j)),
            pl.BlockSpec((seq_len, tc), lambda s, j: (s, CC0 // tc + j)),
            pl.BlockSpec((seq_len, tc), lambda s, j: (s, CX0 // tc + j)),
            pl.BlockSpec((CONV_K, tc), lambda s, j: (0, j)),
        ],
        out_specs=pl.BlockSpec((seq_len, tc), lambda s, j: (s, j)),
        out_shape=jax.ShapeDtypeStruct((n_seq * seq_len, BRANCH_W), BF16),
        compiler_params=_params("parallel", "parallel"),
        name="short_conv",
    )(z, z, z, conv_w)


def _merge_kernel(a0_ref, a1_ref, a2_ref, g0_ref, g1_ref, g2_ref, w_ref, o_ref):
    acc = None
    for k, (a_ref, g_ref) in enumerate(((a0_ref, g0_ref), (a1_ref, g1_ref), (a2_ref, g2_ref))):
        term = jax.nn.sigmoid(g_ref[...].astype(F32)) * _dot(a_ref[...], w_ref[k])
        acc = term if acc is None else acc + term
    o_ref[...] = acc.astype(o_ref.dtype)


def _merge(o_na, o_gqa, y_conv, z, w_branch, tm=512, tn=512):
    m = o_na.shape[0]
    assert G0 % tn == 0 and D_MODEL % tn == 0
    gate_blk = lambda k: (lambda j, i: (i, (G0 + k * D_MODEL) // tn + j))
    act = pl.BlockSpec((tm, BRANCH_W), lambda j, i: (i, 0))
    return pl.pallas_call(
        _merge_kernel,
        grid=(D_MODEL // tn, m // tm),
        in_specs=[act, act, act,
                  pl.BlockSpec((tm, tn), gate_blk(0)),
                  pl.BlockSpec((tm, tn), gate_blk(1)),
                  pl.BlockSpec((tm, tn), gate_blk(2)),
                  pl.BlockSpec((N_BRANCH, BRANCH_W, tn), lambda j, i: (0, 0, j))],
        out_specs=pl.BlockSpec((tm, tn), lambda j, i: (i, j)),
        out_shape=jax.ShapeDtypeStruct((m, D_MODEL), BF16),
        compiler_params=_params("parallel", "parallel"),
        name="merge",
    )(o_na, o_gqa, y_conv, z, z, z, w_branch)


def _residual_norm_inplace(x_ref, gate_ref, g_ref, b_ref, o_ref, alpha):
    gate = gate_ref[...]
    g = g_ref[...]
    b = b_ref[...]

    def body(c, carry):
        r = pl.multiple_of(c * LN_CHUNK, LN_CHUNK)
        y = alpha * x_ref[pl.ds(r, LN_CHUNK), :] + gate * o_ref[pl.ds(r, LN_CHUNK), :]
        o_ref[pl.ds(r, LN_CHUNK), :] = _layer_norm(y) * g + b
        return carry

    lax.fori_loop(0, x_ref.shape[0] // LN_CHUNK, body, 0)


def _out_proj_kernel(m_ref, w_ref, x_ref, gate_ref, g_ref, b_ref, o_ref, *, alpha):
    o_ref[...] = _dot(m_ref[...], w_ref[...])
    _residual_norm_inplace(x_ref, gate_ref, g_ref, b_ref, o_ref, alpha)


def _out_proj(m_act, w_o, x, mod3, gate_blk, ln_g, ln_b, alpha, rows_per_mod, fixed_row, tm=512):
    m, d = x.shape
    row = _mod_row_map(rows_per_mod, tm, fixed_row)
    vec = pl.BlockSpec((1, d), lambda i: (0, 0))
    return pl.pallas_call(
        functools.partial(_out_proj_kernel, alpha=alpha),
        grid=(m // tm,),
        in_specs=[
            pl.BlockSpec((tm, d), lambda i: (i, 0)),
            pl.BlockSpec((d, d), lambda i: (0, 0)),
            pl.BlockSpec((tm, d), lambda i: (i, 0)),
            pl.BlockSpec((None, 1, d), lambda i: (row(i), 0, gate_blk)),
            vec, vec,
        ],
        out_specs=pl.BlockSpec((tm, d), lambda i: (i, 0)),
        out_shape=jax.ShapeDtypeStruct((m, d), F32),
        compiler_params=_params("parallel"),
        name="out_proj",
    )(m_act, w_o, x, mod3, ln_g, ln_b)


def _mlp_kernel(x_ref, sh_ref, sc_ref, gate_ref, g_ref, b_ref, wu_ref, wd_ref, o_ref, h_ref, *, alpha):
    j = pl.program_id(1)

    @pl.when(j == 0)
    def _():
        _modulate_into(x_ref, sh_ref, sc_ref, h_ref)

    u = jnp.maximum(_dot(h_ref[...], wu_ref[...]), 0.0)
    part = _dot((u * u).astype(BF16), wd_ref[...])

    @pl.when(j == 0)
    def _():
        o_ref[...] = part

    @pl.when(j > 0)
    def _():
        o_ref[...] += part

    @pl.when(j == pl.num_programs(1) - 1)
    def _():
        _residual_norm_inplace(x_ref, gate_ref, g_ref, b_ref, o_ref, alpha)


def _mlp(x, mod3, shift_blk, scale_blk, gate_blk, ln_g, ln_b, w_up, w_down, alpha, rows_per_mod, fixed_row,
         tm=512, th=1024):
    m, d = x.shape
    hid = w_up.shape[1]
    row = _mod_row_map(rows_per_mod, tm, fixed_row)
    mod_spec = lambda blk: pl.BlockSpec((None, 1, d), lambda i, j: (row(i), 0, blk))
    vec = pl.BlockSpec((1, d), lambda i, j: (0, 0))
    return pl.pallas_call(
        functools.partial(_mlp_kernel, alpha=alpha),
        grid=(m // tm, hid // th),
        in_specs=[
            pl.BlockSpec((tm, d), lambda i, j: (i, 0)),
            mod_spec(shift_blk), mod_spec(scale_blk), mod_spec(gate_blk),
            vec, vec,
            pl.BlockSpec((d, th), lambda i, j: (0, j)),
            pl.BlockSpec((th, d), lambda i, j: (j, 0)),
        ],
        out_specs=pl.BlockSpec((tm, d), lambda i, j: (i, 0)),
        out_shape=jax.ShapeDtypeStruct((m, d), F32),
        scratch_shapes=[pltpu.VMEM((tm, d), BF16)],
        compiler_params=_params("parallel", "arbitrary"),
        name="mlp",
    )(x, mod3, mod3, mod3, ln_g, ln_b, w_up, w_down)


def _rope_tables(seq):
    t = jnp.arange(seq)
    row = (t // GRID_W).astype(F32)
    col = (t % GRID_W).astype(F32)
    axis_dim = HEAD_DIM // 2
    freqs = ROPE_THETA ** (-jnp.arange(0, axis_dim, 2, dtype=F32) / axis_dim)
    ar, ac = row[:, None] * freqs, col[:, None] * freqs
    cos = jnp.concatenate([jnp.cos(ar), jnp.cos(ar), jnp.cos(ac), jnp.cos(ac)], axis=-1)
    sin = jnp.concatenate([-jnp.sin(ar), jnp.sin(ar), -jnp.sin(ac), jnp.sin(ac)], axis=-1)
    return cos, sin


def kernel(x, c, ctx, c_ctx, w_mod, b_mod, w_in, rpb, q_gain, k_gain, conv_w, w_branch, w_o, w_up, w_down,
           ln_g, ln_b):
    batch, seq, d = x.shape
    ctx_len = ctx.shape[1]
    depth = w_mod.shape[0]
    assert d == D_MODEL and batch < MOD_ROWS and seq % NA_QBLK == 0 and seq // GRID_W >= NA_K_ROWS
    alpha = float((2 * depth) ** 0.25)
    ctx_row = batch

    c_all = jnp.concatenate([c, c_ctx[None, :], jnp.zeros((MOD_ROWS - batch - 1, d), F32)], axis=0)
    mod = _modulation(c_all, w_mod, b_mod)
    cos, sin = _rope_tables(seq)
    plan = _na_plan(seq // GRID_W)

    xl = x.reshape(batch * seq, d)
    xc = ctx.reshape(batch * ctx_len, d)
    for l in range(depth):
        with_ctx_out = l < depth - 1
        mod3 = mod[l].reshape(MOD_ROWS, 1, 6 * d)
        w_in_l = w_in[l].astype(BF16)
        wb_l = w_branch[l].astype(BF16)
        wo_l = w_o[l].astype(BF16)
        wu_l = w_up[l].astype(BF16)
        wd_l = w_down[l].astype(BF16)
        qg, kg = q_gain[l][None, :], k_gain[l][None, :]
        g1, b1 = ln_g[l, 0][None, :], ln_b[l, 0][None, :]
        g2, b2 = ln_g[l, 1][None, :], ln_b[l, 1][None, :]
        bias = _na_bias(rpb[l], plan[2])

        z = _ln_matmul(xl, mod3, 0, 1, w_in_l, seq, None, tm=1024, tn=1536)
        if with_ctx_out:
            zc = _ln_matmul(xc, mod3, 0, 1, w_in_l, None, ctx_row, tm=512, tn=1536)
            ka_c, va_c, kb_c, vb_c = KA0, VA0, KB0, VB0
        else:
            zc = _ln_matmul(xc, mod3, 0, 1, w_in_l[:, KA0:CB0], None, ctx_row, tm=512, tn=(CB0 - KA0) // 2)
            ka_c, va_c, kb_c, vb_c = 0, BRANCH_W, 2 * BRANCH_W, 2 * BRANCH_W + KV_W

        o_na = _na_attention(z, zc, ka_c, va_c, bias, plan, batch, seq, ctx_len)
        o_gqa = _gqa_attention(z, zc, kb_c, vb_c, qg, kg, cos, sin, batch, seq, ctx_len)
        y_conv = _short_conv(z, conv_w[l], batch, seq)
        m_act = _merge(o_na, o_gqa, y_conv, z, wb_l)
        xl = _out_proj(m_act, wo_l, xl, mod3, 2, g1, b1, alpha, seq, None)
        xl = _mlp(xl, mod3, 3, 4, 5, g2, b2, wu_l, wd_l, alpha, seq, None)

        if with_ctx_out:
            o_na_c = _ctx_attention(zc, QA0, KA0, VA0, 1, qg, kg, False, batch, ctx_len)
            o_gqa_c = _ctx_attention(zc, QB0, KB0, VB0, GQA_GROUP, qg, kg, True, batch, ctx_len)
            y_conv_c = _short_conv(zc, conv_w[l], batch, ctx_len)
            m_c = _merge(o_na_c, o_gqa_c, y_conv_c, zc, wb_l)
            xc = _out_proj(m_c, wo_l, xc, mod3, 2, g1, b1, alpha, None, ctx_row)
            xc = _mlp(xc, mod3, 3, 4, 5, g2, b2, wu_l, wd_l, alpha, None, ctx_row)
    return xl.reshape(batch, seq, d)
```

```python
import functools

import numpy as np
import jax
import jax.numpy as jnp
from jax import lax
from jax.experimental import pallas as pl
from jax.experimental.pallas import tpu as pltpu

D_MODEL = 2048
GRID_W = 64
HEAD_DIM = 128
BRANCH_W = D_MODEL // 2
NA_HEADS = BRANCH_W // HEAD_DIM
GQA_HEADS = BRANCH_W // HEAD_DIM
GQA_KV_HEADS = GQA_HEADS // 4
GQA_GROUP = GQA_HEADS // GQA_KV_HEADS
KV_W = GQA_KV_HEADS * HEAD_DIM
CONV_K = 3
MLP_HIDDEN = 4 * D_MODEL
N_BRANCH = 3
WIN_R = 8
WIN_C = 16
ROPE_THETA = 10000.0
NORM_EPS = 1e-6
NEG_INF = -1e30
ATTN_SCALE = HEAD_DIM ** -0.5
LOG2E = 1.4426950408889634
SCORE_SCALE = ATTN_SCALE * LOG2E

_W_IN_SECTIONS = (("qa", BRANCH_W), ("qb", BRANCH_W), ("ka", BRANCH_W), ("va", BRANCH_W), ("kb", KV_W),
                  ("vb", KV_W), ("cb", BRANCH_W), ("cc", BRANCH_W), ("cx", BRANCH_W), ("g", N_BRANCH * D_MODEL))
_Z_ORDER = ("g", "qa", "qb", "ka", "va", "cb", "cc", "cx", "kb", "vb")


def _offsets(names):
    width = dict(_W_IN_SECTIONS)
    out, pos = {}, 0
    for name in names:
        out[name] = pos
        pos += width[name]
    return out, pos


_W_OFF, IN_W = _offsets([name for name, _ in _W_IN_SECTIONS])
_Z_OFF, _ = _offsets(_Z_ORDER)
G0, QA0, QB0, KA0, VA0 = (_Z_OFF[k] for k in ("g", "qa", "qb", "ka", "va"))
CB0, CC0, CX0, KB0, VB0 = (_Z_OFF[k] for k in ("cb", "cc", "cx", "kb", "vb"))


def _w_in_columns(w, names):
    width = dict(_W_IN_SECTIONS)
    return jnp.concatenate([w[:, _W_OFF[k]:_W_OFF[k] + width[k]] for k in names], axis=1)

MOD_ROWS = 32
NA_Q_ROWS = 4
NA_K_ROWS = 12
NA_QBLK = NA_Q_ROWS * GRID_W
NA_KWIN = NA_K_ROWS * GRID_W

VMEM_LIMIT_BYTES = 56 * 1024 * 1024
LN_CHUNK = 128

F32 = jnp.float32
BF16 = jnp.bfloat16


def _params(*semantics):
    return pltpu.CompilerParams(dimension_semantics=semantics, vmem_limit_bytes=VMEM_LIMIT_BYTES)


def _layer_norm(x):
    mu = jnp.mean(x, axis=-1, keepdims=True)
    xc = x - mu
    var = jnp.mean(xc * xc, axis=-1, keepdims=True)
    return xc * lax.rsqrt(var + NORM_EPS)


def _rms_norm(x, gain):
    return x * lax.rsqrt(jnp.mean(x * x, axis=-1, keepdims=True) + NORM_EPS) * gain


def _rope(x, cos, sin_signed):
    lane = lax.broadcasted_iota(jnp.int32, x.shape, 1)
    partner = jnp.where((lane & 32) == 0, pltpu.roll(x, 96, axis=1), pltpu.roll(x, 32, axis=1))
    return x * cos + partner * sin_signed


def _dot(a, b):
    return jnp.dot(a, b, preferred_element_type=F32)


def _dot_nt(a, b):
    return lax.dot_general(a, b, (((1,), (1,)), ((), ())), preferred_element_type=F32)


def _mod_kernel(c_ref, w_ref, b_ref, o_ref):
    c = c_ref[...]
    a = (c * jax.nn.sigmoid(c)).astype(BF16)
    o_ref[...] = _dot(a, w_ref[...].astype(BF16)) + b_ref[...]


def _modulation(c_all, w_mod, b_mod, tn=1024):
    depth, d, n = w_mod.shape
    return pl.pallas_call(
        _mod_kernel,
        grid=(depth, n // tn),
        in_specs=[
            pl.BlockSpec((MOD_ROWS, d), lambda l, j: (0, 0)),
            pl.BlockSpec((None, d, tn), lambda l, j: (l, 0, j)),
            pl.BlockSpec((None, 1, tn), lambda l, j: (l, 0, j)),
        ],
        out_specs=pl.BlockSpec((None, MOD_ROWS, tn), lambda l, j: (l, 0, j)),
        out_shape=jax.ShapeDtypeStruct((depth, MOD_ROWS, n), F32),
        compiler_params=_params("parallel", "parallel"),
        name="modulation",
    )(c_all, w_mod, b_mod.reshape(depth, 1, n))


def _modulate_rows(x_ref, sh, sc1, h_ref, r0, rows):
    for s in range(rows // LN_CHUNK):
        rr = slice(r0 + s * LN_CHUNK, r0 + (s + 1) * LN_CHUNK)
        h_ref[rr, :] = (_layer_norm(x_ref[rr, :]) * sc1 + sh).astype(h_ref.dtype)


def _residual_norm_rows(x_ref, f, gate, g, b, o_ref, r0, alpha):
    for s in range(f.shape[0] // LN_CHUNK):
        rr = slice(r0 + s * LN_CHUNK, r0 + (s + 1) * LN_CHUNK)
        y = alpha * x_ref[rr, :] + gate * f[s * LN_CHUNK:(s + 1) * LN_CHUNK, :]
        o_ref[rr, :] = _layer_norm(y) * g + b


def _ln_matmul_kernel(x_ref, sh_ref, sc_ref, w_ref, o_ref, h_ref, *, rows):
    j = pl.program_id(1)

    @pl.when(j == 0)
    def _():
        sc1 = 1.0 + sc_ref[...]
        sh = sh_ref[...]
        for c in range(x_ref.shape[0] // rows):
            _modulate_rows(x_ref, sh, sc1, h_ref, c * rows, rows)
            r = slice(c * rows, (c + 1) * rows)
            o_ref[r, :] = _dot(h_ref[r, :], w_ref[...]).astype(o_ref.dtype)

    @pl.when(j > 0)
    def _():
        o_ref[...] = _dot(h_ref[...], w_ref[...]).astype(o_ref.dtype)


def _mod_row_map(rows_per_mod, tm, fixed_row):
    if fixed_row is not None:
        return lambda i: fixed_row
    return lambda i: (i * tm) // rows_per_mod


def _ln_matmul(x, mod3, shift_blk, scale_blk, w, rows_per_mod, fixed_row, tm, tn, rows=256):
    m, d = x.shape
    n = w.shape[1]
    row = _mod_row_map(rows_per_mod, tm, fixed_row)
    return pl.pallas_call(
        functools.partial(_ln_matmul_kernel, rows=rows),
        grid=(m // tm, n // tn),
        in_specs=[
            pl.BlockSpec((tm, d), lambda i, j: (i, 0)),
            pl.BlockSpec((None, 1, d), lambda i, j: (row(i), 0, shift_blk)),
            pl.BlockSpec((None, 1, d), lambda i, j: (row(i), 0, scale_blk)),
            pl.BlockSpec((d, tn), lambda i, j: (0, j)),
        ],
        out_specs=pl.BlockSpec((tm, tn), lambda i, j: (i, j)),
        out_shape=jax.ShapeDtypeStruct((m, n), BF16),
        scratch_shapes=[pltpu.VMEM((tm, d), BF16)],
        compiler_params=_params("parallel", "arbitrary"),
        name="ln_matmul",
    )(x, mod3, mod3, w)


def _na_plan(n_rows):
    wr = min(WIN_R, n_rows)
    n_qblk = n_rows // NA_Q_ROWS
    r = np.arange(n_rows)
    row_start = np.clip(r - wr // 2, 0, n_rows - wr)
    starts, pat_ids, patterns, keys = [], [], [], {}
    for qi in range(n_qblk):
        k0 = int(np.clip(qi * NA_Q_ROWS - NA_Q_ROWS, 0, n_rows - NA_K_ROWS))
        q_rows = qi * NA_Q_ROWS + np.arange(NA_Q_ROWS)
        k_rows = k0 + np.arange(NA_K_ROWS)
        valid = (k_rows[None, :] >= row_start[q_rows][:, None]) & (k_rows[None, :] < row_start[q_rows][:, None] + wr)
        assert valid.sum(axis=1).min() == wr, "key window must cover every query row's neighbourhood"
        dr_idx = np.clip(k_rows[None, :] - q_rows[:, None] + WIN_R - 1, 0, 2 * WIN_R - 2)
        key = (valid.tobytes(), dr_idx.tobytes())
        if key not in keys:
            keys[key] = len(patterns)
            patterns.append((valid, dr_idx))
        starts.append(k0)
        pat_ids.append(keys[key])
    return starts, pat_ids, patterns


def _na_bias(rpb, patterns):
    heads, n_dr, n_dc = rpb.shape
    col = np.arange(GRID_W)
    col_start = np.clip(col - WIN_C // 2, 0, GRID_W - WIN_C)
    in_win = (col[None, :] >= col_start[:, None]) & (col[None, :] < col_start[:, None] + WIN_C)
    assert np.abs(col[None, :] - col[:, None])[in_win].max() <= WIN_C - 1
    line = 2 * GRID_W
    left = GRID_W - 1 - (WIN_C - 1)
    w = jnp.pad(rpb.astype(F32), ((0, 0), (0, 0), (left, line - left - n_dc)))
    w = jnp.broadcast_to(w[:, :, None, :], (heads, n_dr, GRID_W, line)).reshape(heads, n_dr, GRID_W * line)
    toe = w[:, :, :GRID_W * (line - 1)].reshape(heads, n_dr, GRID_W, line - 1)[..., GRID_W - 1:]
    toe = jnp.where(in_win[None, None], toe * LOG2E, NEG_INF)
    out = []
    for valid, dr_idx in patterns:
        blocks = jnp.stack([toe[:, int(i)] for i in dr_idx.reshape(-1)], axis=1)
        blocks = blocks.reshape(heads, NA_Q_ROWS, NA_K_ROWS, GRID_W, GRID_W)
        blocks = jnp.where(valid[None, :, :, None, None], blocks, NEG_INF)
        out.append(blocks.transpose(0, 1, 3, 2, 4).reshape(heads, NA_QBLK, NA_KWIN))
    return jnp.stack(out, axis=1)


def _fill_values_with_ones(vs_ref, v_ref, vc_ref, seq):
    vs_ref[0:seq, 0:HEAD_DIM] = v_ref[...]
    vs_ref[seq:, 0:HEAD_DIM] = vc_ref[...]
    vs_ref[:, HEAD_DIM:] = jnp.ones((vs_ref.shape[0], HEAD_DIM), vs_ref.dtype)


def _normalised(o_aug):
    return o_aug[:, :HEAD_DIM] / o_aug[:, HEAD_DIM:]


def _na_kernel(q_ref, k_ref, v_ref, kc_ref, vc_ref, bias_ref, o_ref, vs_ref, *, starts, pat_ids, seq):
    _fill_values_with_ones(vs_ref, v_ref, vc_ref, seq)
    kc = kc_ref[...]
    for qi, (k0, pat) in enumerate(zip(starts, pat_ids)):
        q = q_ref[qi * NA_QBLK:(qi + 1) * NA_QBLK, :]
        kw = k_ref[k0 * GRID_W:k0 * GRID_W + NA_KWIN, :]
        s_loc = _dot_nt(q, kw) * SCORE_SCALE + bias_ref[pat]
        s_ctx = _dot_nt(q, kc) * SCORE_SCALE
        m = jnp.maximum(jnp.max(s_loc, axis=-1, keepdims=True), jnp.max(s_ctx, axis=-1, keepdims=True))
        p_loc = jnp.exp2(s_loc - m).astype(BF16)
        p_ctx = jnp.exp2(s_ctx - m).astype(BF16)
        o_aug = _dot(p_loc, vs_ref[k0 * GRID_W:k0 * GRID_W + NA_KWIN, :]) + _dot(p_ctx, vs_ref[seq:, :])
        o_ref[qi * NA_QBLK:(qi + 1) * NA_QBLK, :] = _normalised(o_aug).astype(o_ref.dtype)


def _na_attention(z, zc, kc_col, vc_col, bias, plan, batch, seq, ctx_len):
    starts, pat_ids, patterns = plan
    hb = HEAD_DIM
    kernel = functools.partial(_na_kernel, starts=tuple(starts), pat_ids=tuple(pat_ids), seq=seq)
    return pl.pallas_call(
        kernel,
        grid=(NA_HEADS, batch),
        in_specs=[
            pl.BlockSpec((seq, hb), lambda h, b: (b, QA0 // hb + h)),
            pl.BlockSpec((seq, hb), lambda h, b: (b, KA0 // hb + h)),
            pl.BlockSpec((seq, hb), lambda h, b: (b, VA0 // hb + h)),
            pl.BlockSpec((ctx_len, hb), lambda h, b: (b, kc_col // hb + h)),
            pl.BlockSpec((ctx_len, hb), lambda h, b: (b, vc_col // hb + h)),
            pl.BlockSpec((None, len(patterns), NA_QBLK, NA_KWIN), lambda h, b: (h, 0, 0, 0)),
        ],
        out_specs=pl.BlockSpec((seq, hb), lambda h, b: (b, h)),
        out_shape=jax.ShapeDtypeStruct((batch * seq, BRANCH_W), BF16),
        scratch_shapes=[pltpu.VMEM((seq + ctx_len, 2 * hb), BF16)],
        compiler_params=_params("parallel", "parallel"),
        name="na_attention",
    )(z, z, z, zc, zc, bias)


def _gqa_kernel(q_ref, k_ref, v_ref, kc_ref, vc_ref, qg_ref, kg_ref, cos_ref, sin_ref, o_ref,
                ks_ref, vs_ref, *, tq, seq):
    qi = pl.program_id(2)

    @pl.when(qi == 0)
    def _():
        kg = kg_ref[...]
        kn = _rms_norm(k_ref[...].astype(F32), kg)
        ks_ref[0:seq, :] = _rope(kn, cos_ref[...], sin_ref[...]).astype(BF16)
        ks_ref[seq:, :] = _rms_norm(kc_ref[...].astype(F32), kg).astype(BF16)
        _fill_values_with_ones(vs_ref, v_ref, vc_ref, seq)

    r = pl.multiple_of(qi * tq, tq)
    cos = cos_ref[pl.ds(r, tq), :]
    sin = sin_ref[pl.ds(r, tq), :]
    qg = qg_ref[...]
    for g in range(GQA_GROUP):
        q = q_ref[:, g * HEAD_DIM:(g + 1) * HEAD_DIM].astype(F32)
        qn = (_rope(_rms_norm(q, qg), cos, sin) * SCORE_SCALE).astype(BF16)
        s = _dot_nt(qn, ks_ref[...])
        m = jnp.max(s, axis=-1, keepdims=True)
        p = jnp.exp2(s - m).astype(BF16)
        o_aug = _dot(p, vs_ref[...])
        o_ref[:, g * HEAD_DIM:(g + 1) * HEAD_DIM] = _normalised(o_aug).astype(o_ref.dtype)


def _gqa_attention(z, zc, kc_col, vc_col, q_gain, k_gain, cos, sin, batch, seq, ctx_len, tq=256):
    hb = HEAD_DIM
    gw = GQA_GROUP * HEAD_DIM
    nq = seq // tq
    kernel = functools.partial(_gqa_kernel, tq=tq, seq=seq)
    return pl.pallas_call(
        kernel,
        grid=(batch, GQA_KV_HEADS, nq),
        in_specs=[
            pl.BlockSpec((tq, gw), lambda b, kh, qi: (b * nq + qi, QB0 // gw + kh)),
            pl.BlockSpec((seq, hb), lambda b, kh, qi: (b, KB0 // hb + kh)),
            pl.BlockSpec((seq, hb), lambda b, kh, qi: (b, VB0 // hb + kh)),
            pl.BlockSpec((ctx_len, hb), lambda b, kh, qi: (b, kc_col // hb + kh)),
            pl.BlockSpec((ctx_len, hb), lambda b, kh, qi: (b, vc_col // hb + kh)),
            pl.BlockSpec((1, hb), lambda b, kh, qi: (0, 0)),
            pl.BlockSpec((1, hb), lambda b, kh, qi: (0, 0)),
            pl.BlockSpec((seq, hb), lambda b, kh, qi: (0, 0)),
            pl.BlockSpec((seq, hb), lambda b, kh, qi: (0, 0)),
        ],
        out_specs=pl.BlockSpec((tq, gw), lambda b, kh, qi: (b * nq + qi, kh)),
        out_shape=jax.ShapeDtypeStruct((batch * seq, BRANCH_W), BF16),
        scratch_shapes=[pltpu.VMEM((seq + ctx_len, hb), BF16), pltpu.VMEM((seq + ctx_len, 2 * hb), BF16)],
        compiler_params=_params("parallel", "parallel", "arbitrary"),
        name="gqa_attention",
    )(z, z, z, zc, zc, q_gain, k_gain, cos, sin)


def _ctx_attn_kernel(q_ref, k_ref, v_ref, qg_ref, kg_ref, o_ref, *, qk_norm):
    q = q_ref[...]
    k = k_ref[...]
    if qk_norm:
        q = _rms_norm(q.astype(F32), qg_ref[...]).astype(BF16)
        k = _rms_norm(k.astype(F32), kg_ref[...]).astype(BF16)
    s = _dot_nt(q, k) * ATTN_SCALE
    m = jnp.max(s, axis=-1, keepdims=True)
    p = jnp.exp(s - m)
    den = jnp.sum(p, axis=-1, keepdims=True)
    o_ref[...] = (_dot(p.astype(BF16), v_ref[...]) / den).astype(o_ref.dtype)


def _ctx_attention(zc, q_col, k_col, v_col, kv_group, q_gain, k_gain, qk_norm, batch, ctx_len):
    hb = HEAD_DIM
    kernel = functools.partial(_ctx_attn_kernel, qk_norm=qk_norm)
    return pl.pallas_call(
        kernel,
        grid=(batch, BRANCH_W // hb),
        in_specs=[
            pl.BlockSpec((ctx_len, hb), lambda b, h: (b, q_col // hb + h)),
            pl.BlockSpec((ctx_len, hb), lambda b, h: (b, k_col // hb + h // kv_group)),
            pl.BlockSpec((ctx_len, hb), lambda b, h: (b, v_col // hb + h // kv_group)),
            pl.BlockSpec((1, hb), lambda b, h: (0, 0)),
            pl.BlockSpec((1, hb), lambda b, h: (0, 0)),
        ],
        out_specs=pl.BlockSpec((ctx_len, hb), lambda b, h: (b, h)),
        out_shape=jax.ShapeDtypeStruct((batch * ctx_len, BRANCH_W), BF16),
        compiler_params=_params("parallel", "parallel"),
        name="ctx_attention",
    )(zc, zc, zc, q_gain, k_gain)


def _conv_kernel(b_ref, c_ref, u_ref, w_ref, o_ref):
    v = c_ref[...].astype(F32) * u_ref[...].astype(F32)
    n = v.shape[0]
    row = lax.broadcasted_iota(jnp.int32, v.shape, 0)
    prev = jnp.where(row == 0, 0.0, pltpu.roll(v, 1, axis=0))
    nxt = jnp.where(row == n - 1, 0.0, pltpu.roll(v, n - 1, axis=0))
    w = w_ref[...]
    y = w[0:1, :] * prev + w[1:2, :] * v + w[2:3, :] * nxt
    o_ref[...] = (b_ref[...].astype(F32) * y).astype(o_ref.dtype)


def _short_conv(z, conv_w, n_seq, seq_len, tc=256):
    return pl.pallas_call(
        _conv_kernel,
        grid=(n_seq, BRANCH_W // tc),
        in_specs=[
            pl.BlockSpec((seq_len, tc), lambda s, j: (s, CB0 // tc + j)),
            pl.BlockSpec((seq_len, tc), lambda s, j: (s, CC0 // tc + j)),
            pl.BlockSpec((seq_len, tc), lambda s, j: (s, CX0 // tc + j)),
            pl.BlockSpec((CONV_K, tc), lambda s, j: (0, j)),
        ],
        out_specs=pl.BlockSpec((seq_len, tc), lambda s, j: (s, j)),
        out_shape=jax.ShapeDtypeStruct((n_seq * seq_len, BRANCH_W), BF16),
        compiler_params=_params("parallel", "parallel"),
        name="short_conv",
    )(z, z, z, conv_w)


def _resident(block_shape, index_map):
    return pl.BlockSpec(block_shape, index_map, pipeline_mode=pl.Buffered(1))


def _merge_kernel(a0_ref, a1_ref, a2_ref, g_ref, w_ref, o_ref, *, tn):
    d = o_ref.shape[1]
    for j in range(d // tn):
        cols = slice(j * tn, (j + 1) * tn)
        acc = None
        for k, a_ref in enumerate((a0_ref, a1_ref, a2_ref)):
            gate = jax.nn.sigmoid(g_ref[:, k * d + j * tn:k * d + (j + 1) * tn].astype(F32))
            term = gate * _dot(a_ref[...], w_ref[k, :, cols])
            acc = term if acc is None else acc + term
        o_ref[:, cols] = acc.astype(o_ref.dtype)


def _merge(o_na, o_gqa, y_conv, z, w_branch, tm=512, tn=512):
    m = o_na.shape[0]
    gw = N_BRANCH * D_MODEL
    assert G0 % gw == 0
    act = pl.BlockSpec((tm, BRANCH_W), lambda i: (i, 0))
    return pl.pallas_call(
        functools.partial(_merge_kernel, tn=tn),
        grid=(m // tm,),
        in_specs=[act, act, act,
                  pl.BlockSpec((tm, gw), lambda i: (i, G0 // gw)),
                  _resident((N_BRANCH, BRANCH_W, D_MODEL), lambda i: (0, 0, 0))],
        out_specs=pl.BlockSpec((tm, D_MODEL), lambda i: (i, 0)),
        out_shape=jax.ShapeDtypeStruct((m, D_MODEL), BF16),
        compiler_params=_params("parallel"),
        name="merge",
    )(o_na, o_gqa, y_conv, z, w_branch)


def _out_proj_kernel(m_ref, w_ref, x_ref, gate_ref, g_ref, b_ref, o_ref, *, alpha, rows):
    gate = gate_ref[...]
    g = g_ref[...]
    b = b_ref[...]
    for c in range(o_ref.shape[0] // rows):
        f = _dot(m_ref[c * rows:(c + 1) * rows, :], w_ref[...])
        _residual_norm_rows(x_ref, f, gate, g, b, o_ref, c * rows, alpha)


def _out_proj(m_act, w_o, x, mod3, gate_blk, ln_g, ln_b, alpha, rows_per_mod, fixed_row, tm=512, rows=256):
    m, d = x.shape
    row = _mod_row_map(rows_per_mod, tm, fixed_row)
    vec = pl.BlockSpec((1, d), lambda i: (0, 0))
    return pl.pallas_call(
        functools.partial(_out_proj_kernel, alpha=alpha, rows=rows),
        grid=(m // tm,),
        in_specs=[
            pl.BlockSpec((tm, d), lambda i: (i, 0)),
            _resident((d, d), lambda i: (0, 0)),
            pl.BlockSpec((tm, d), lambda i: (i, 0)),
            pl.BlockSpec((None, 1, d), lambda i: (row(i), 0, gate_blk)),
            vec, vec,
        ],
        out_specs=pl.BlockSpec((tm, d), lambda i: (i, 0)),
        out_shape=jax.ShapeDtypeStruct((m, d), F32),
        compiler_params=_params("parallel"),
        name="out_proj",
    )(m_act, w_o, x, mod3, ln_g, ln_b)


def _mlp_kernel(x_ref, sh_ref, sc_ref, gate_ref, g_ref, b_ref, wu_ref, wd_ref, o_ref, h_ref, *, alpha, rows):
    j = pl.program_id(1)
    last = pl.num_programs(1) - 1
    n_chunks = o_ref.shape[0] // rows

    def down(r):
        u = jnp.maximum(_dot(h_ref[r, :], wu_ref[...]), 0.0)
        return _dot((u * u).astype(BF16), wd_ref[...])

    @pl.when(j == 0)
    def _():
        sc1 = 1.0 + sc_ref[...]
        sh = sh_ref[...]
        for c in range(n_chunks):
            _modulate_rows(x_ref, sh, sc1, h_ref, c * rows, rows)
            r = slice(c * rows, (c + 1) * rows)
            o_ref[r, :] = down(r)

    @pl.when(jnp.logical_and(j > 0, j < last))
    def _():
        o_ref[...] += down(slice(None))

    @pl.when(j == last)
    def _():
        gate = gate_ref[...]
        g = g_ref[...]
        b = b_ref[...]
        for c in range(n_chunks):
            r = slice(c * rows, (c + 1) * rows)
            _residual_norm_rows(x_ref, o_ref[r, :] + down(r), gate, g, b, o_ref, c * rows, alpha)


def _mlp(x, mod3, shift_blk, scale_blk, gate_blk, ln_g, ln_b, w_up, w_down, alpha, rows_per_mod, fixed_row,
         tm=512, th=1024, rows=256):
    m, d = x.shape
    hid = w_up.shape[1]
    assert hid // th >= 2
    row = _mod_row_map(rows_per_mod, tm, fixed_row)
    mod_spec = lambda blk: pl.BlockSpec((None, 1, d), lambda i, j: (row(i), 0, blk))
    vec = pl.BlockSpec((1, d), lambda i, j: (0, 0))
    return pl.pallas_call(
        functools.partial(_mlp_kernel, alpha=alpha, rows=rows),
        grid=(m // tm, hid // th),
        in_specs=[
            pl.BlockSpec((tm, d), lambda i, j: (i, 0)),
            mod_spec(shift_blk), mod_spec(scale_blk), mod_spec(gate_blk),
            vec, vec,
            pl.BlockSpec((d, th), lambda i, j: (0, j)),
            pl.BlockSpec((th, d), lambda i, j: (j, 0)),
        ],
        out_specs=pl.BlockSpec((tm, d), lambda i, j: (i, 0)),
        out_shape=jax.ShapeDtypeStruct((m, d), F32),
        scratch_shapes=[pltpu.VMEM((tm, d), BF16)],
        compiler_params=_params("parallel", "arbitrary"),
        name="mlp",
    )(x, mod3, mod3, mod3, ln_g, ln_b, w_up, w_down)


def _rope_tables(seq):
    t = jnp.arange(seq)
    row = (t // GRID_W).astype(F32)
    col = (t % GRID_W).astype(F32)
    axis_dim = HEAD_DIM // 2
    freqs = ROPE_THETA ** (-jnp.arange(0, axis_dim, 2, dtype=F32) / axis_dim)
    ar, ac = row[:, None] * freqs, col[:, None] * freqs
    cos = jnp.concatenate([jnp.cos(ar), jnp.cos(ar), jnp.cos(ac), jnp.cos(ac)], axis=-1)
    sin = jnp.concatenate([-jnp.sin(ar), jnp.sin(ar), -jnp.sin(ac), jnp.sin(ac)], axis=-1)
    return cos, sin


def kernel(x, c, ctx, c_ctx, w_mod, b_mod, w_in, rpb, q_gain, k_gain, conv_w, w_branch, w_o, w_up, w_down,
           ln_g, ln_b):
    batch, seq, d = x.shape
    ctx_len = ctx.shape[1]
    depth = w_mod.shape[0]
    assert d == D_MODEL and batch < MOD_ROWS and seq % NA_QBLK == 0 and seq // GRID_W >= NA_K_ROWS
    alpha = float((2 * depth) ** 0.25)
    ctx_row = batch

    c_all = jnp.concatenate([c, c_ctx[None, :], jnp.zeros((MOD_ROWS - batch - 1, d), F32)], axis=0)
    mod = _modulation(c_all, w_mod, b_mod)
    cos, sin = _rope_tables(seq)
    plan = _na_plan(seq // GRID_W)

    xl = x.reshape(batch * seq, d)
    xc = ctx.reshape(batch * ctx_len, d)
    for l in range(depth):
        with_ctx_out = l < depth - 1
        mod3 = mod[l].reshape(MOD_ROWS, 1, 6 * d)
        w_in_l = _w_in_columns(w_in[l], _Z_ORDER).astype(BF16)
        wb_l = w_branch[l].astype(BF16)
        wo_l = w_o[l].astype(BF16)
        wu_l = w_up[l].astype(BF16)
        wd_l = w_down[l].astype(BF16)
        qg, kg = q_gain[l][None, :], k_gain[l][None, :]
        g1, b1 = ln_g[l, 0][None, :], ln_b[l, 0][None, :]
        g2, b2 = ln_g[l, 1][None, :], ln_b[l, 1][None, :]
        bias = _na_bias(rpb[l], plan[2])

        z = _ln_matmul(xl, mod3, 0, 1, w_in_l, seq, None, tm=1024, tn=1536)
        if with_ctx_out:
            zc = _ln_matmul(xc, mod3, 0, 1, w_in_l, None, ctx_row, tm=512, tn=1536)
            ka_c, va_c, kb_c, vb_c = KA0, VA0, KB0, VB0
        else:
            w_kv = _w_in_columns(w_in[l], ("ka", "va", "kb", "vb")).astype(BF16)
            zc = _ln_matmul(xc, mod3, 0, 1, w_kv, None, ctx_row, tm=512, tn=w_kv.shape[1] // 2)
            ka_c, va_c, kb_c, vb_c = 0, BRANCH_W, 2 * BRANCH_W, 2 * BRANCH_W + KV_W

        o_na = _na_attention(z, zc, ka_c, va_c, bias, plan, batch, seq, ctx_len)
        o_gqa = _gqa_attention(z, zc, kb_c, vb_c, qg, kg, cos, sin, batch, seq, ctx_len)
        y_conv = _short_conv(z, conv_w[l], batch, seq)
        m_act = _merge(o_na, o_gqa, y_conv, z, wb_l)
        xl = _out_proj(m_act, wo_l, xl, mod3, 2, g1, b1, alpha, seq, None)
        xl = _mlp(xl, mod3, 3, 4, 5, g2, b2, wu_l, wd_l, alpha, seq, None)

        if with_ctx_out:
            o_na_c = _ctx_attention(zc, QA0, KA0, VA0, 1, qg, kg, False, batch, ctx_len)
            o_gqa_c = _ctx_attention(zc, QB0, KB0, VB0, GQA_GROUP, qg, kg, True, batch, ctx_len)
            y_conv_c = _short_conv(zc, conv_w[l], batch, ctx_len)
            m_c = _merge(o_na_c, o_gqa_c, y_conv_c, zc, wb_l)
            xc = _out_proj(m_c, wo_l, xc, mod3, 2, g1, b1, alpha, None, ctx_row)
            xc = _mlp(xc, mod3, 3, 4, 5, g2, b2, wu_l, wd_l, alpha, None, ctx_row)
    return xl.reshape(batch, seq, d)
```

```python
import functools

import numpy as np
import jax
import jax.numpy as jnp
from jax import lax
from jax.experimental import pallas as pl
from jax.experimental.pallas import tpu as pltpu

D_MODEL = 2048
GRID_W = 64
HEAD_DIM = 128
BRANCH_W = D_MODEL // 2
NA_HEADS = BRANCH_W // HEAD_DIM
GQA_HEADS = BRANCH_W // HEAD_DIM
GQA_KV_HEADS = GQA_HEADS // 4
GQA_GROUP = GQA_HEADS // GQA_KV_HEADS
KV_W = GQA_KV_HEADS * HEAD_DIM
CONV_K = 3
MLP_HIDDEN = 4 * D_MODEL
N_BRANCH = 3
WIN_R = 8
WIN_C = 16
ROPE_THETA = 10000.0
NORM_EPS = 1e-6
NEG_INF = -1e30
ATTN_SCALE = HEAD_DIM ** -0.5
LOG2E = 1.4426950408889634
SCORE_SCALE = ATTN_SCALE * LOG2E

_W_IN_SECTIONS = (("qa", BRANCH_W), ("qb", BRANCH_W), ("ka", BRANCH_W), ("va", BRANCH_W), ("kb", KV_W),
                  ("vb", KV_W), ("cb", BRANCH_W), ("cc", BRANCH_W), ("cx", BRANCH_W), ("g", N_BRANCH * D_MODEL))
_Z_ORDER = ("g", "qa", "qb", "ka", "va", "cb", "cc", "cx", "kb", "vb")


def _offsets(names):
    width = dict(_W_IN_SECTIONS)
    out, pos = {}, 0
    for name in names:
        out[name] = pos
        pos += width[name]
    return out, pos


_W_OFF, IN_W = _offsets([name for name, _ in _W_IN_SECTIONS])
_Z_OFF, _ = _offsets(_Z_ORDER)
G0, QA0, QB0, KA0, VA0 = (_Z_OFF[k] for k in ("g", "qa", "qb", "ka", "va"))
CB0, CC0, CX0, KB0, VB0 = (_Z_OFF[k] for k in ("cb", "cc", "cx", "kb", "vb"))


def _w_in_columns(w, names):
    width = dict(_W_IN_SECTIONS)
    return jnp.concatenate([w[..., _W_OFF[k]:_W_OFF[k] + width[k]] for k in names], axis=-1)


MOD_ROWS = 32
NA_Q_ROWS = 4
NA_K_ROWS = 12
NA_QBLK = NA_Q_ROWS * GRID_W
NA_KWIN = NA_K_ROWS * GRID_W

VMEM_LIMIT_BYTES = 56 * 1024 * 1024
LN_CHUNK = 128

F32 = jnp.float32
BF16 = jnp.bfloat16


def _params(*semantics):
    return pltpu.CompilerParams(dimension_semantics=semantics, vmem_limit_bytes=VMEM_LIMIT_BYTES)


def _resident(block_shape, index_map):
    return pl.BlockSpec(block_shape, index_map, pipeline_mode=pl.Buffered(1))


def _layer_norm(x):
    mu = jnp.mean(x, axis=-1, keepdims=True)
    xc = x - mu
    var = jnp.mean(xc * xc, axis=-1, keepdims=True)
    return xc * lax.rsqrt(var + NORM_EPS)


def _rms_norm(x, gain):
    return x * lax.rsqrt(jnp.mean(x * x, axis=-1, keepdims=True) + NORM_EPS) * gain


def _rope(x, cos, sin_signed):
    lane = lax.broadcasted_iota(jnp.int32, x.shape, 1)
    partner = jnp.where((lane & 32) == 0, pltpu.roll(x, 96, axis=1), pltpu.roll(x, 32, axis=1))
    return x * cos + partner * sin_signed


def _dot(a, b):
    return jnp.dot(a, b, preferred_element_type=F32)


def _dot_nt(a, b):
    return lax.dot_general(a, b, (((1,), (1,)), ((), ())), preferred_element_type=F32)


def _modulate_rows(x_ref, sh, sc1, h_ref, r0, rows):
    for s in range(rows // LN_CHUNK):
        rr = slice(r0 + s * LN_CHUNK, r0 + (s + 1) * LN_CHUNK)
        h_ref[rr, :] = (_layer_norm(x_ref[rr, :]) * sc1 + sh).astype(h_ref.dtype)


def _residual_norm_rows(x_ref, f, gate, g, b, o_ref, r0, alpha):
    for s in range(f.shape[0] // LN_CHUNK):
        rr = slice(r0 + s * LN_CHUNK, r0 + (s + 1) * LN_CHUNK)
        y = alpha * x_ref[rr, :] + gate * f[s * LN_CHUNK:(s + 1) * LN_CHUNK, :]
        o_ref[rr, :] = _layer_norm(y) * g + b


def _mod_row_map(layer, rows_per_mod, tm, fixed_row):
    base = layer * MOD_ROWS
    if fixed_row is not None:
        return lambda i: base + fixed_row
    return lambda i: base + (i * tm) // rows_per_mod


def _mod_kernel(c_ref, w_ref, b_ref, o_ref):
    c = c_ref[...]
    a = (c * jax.nn.sigmoid(c)).astype(BF16)
    o_ref[...] = _dot(a, w_ref[...].astype(BF16)) + b_ref[...]


def _modulation(c_all, w_mod, b_mod, tn=1024):
    depth, d, n = w_mod.shape
    return pl.pallas_call(
        _mod_kernel,
        grid=(depth, n // tn),
        in_specs=[
            pl.BlockSpec((MOD_ROWS, d), lambda l, j: (0, 0)),
            pl.BlockSpec((None, d, tn), lambda l, j: (l, 0, j)),
            pl.BlockSpec((None, 1, tn), lambda l, j: (l, 0, j)),
        ],
        out_specs=pl.BlockSpec((None, MOD_ROWS, tn), lambda l, j: (l, 0, j)),
        out_shape=jax.ShapeDtypeStruct((depth, MOD_ROWS, n), F32),
        compiler_params=_params("parallel", "parallel"),
        name="modulation",
    )(c_all, w_mod, b_mod.reshape(depth, 1, n))


def _ln_matmul_kernel(x_ref, sh_ref, sc_ref, w_ref, o_ref, h_ref, *, rows):
    j = pl.program_id(1)

    @pl.when(j == 0)
    def _():
        sc1 = 1.0 + sc_ref[...]
        sh = sh_ref[...]
        for c in range(x_ref.shape[0] // rows):
            _modulate_rows(x_ref, sh, sc1, h_ref, c * rows, rows)
            r = slice(c * rows, (c + 1) * rows)
            o_ref[r, :] = _dot(h_ref[r, :], w_ref[...]).astype(o_ref.dtype)

    @pl.when(j > 0)
    def _():
        o_ref[...] = _dot(h_ref[...], w_ref[...]).astype(o_ref.dtype)


def _ln_matmul(x, mod3, layer, shift_blk, scale_blk, w, w_layer, rows_per_mod, fixed_row, tm, tn, rows=256):
    m, d = x.shape
    n = w.shape[2]
    row = _mod_row_map(layer, rows_per_mod, tm, fixed_row)
    return pl.pallas_call(
        functools.partial(_ln_matmul_kernel, rows=rows),
        grid=(m // tm, n // tn),
        in_specs=[
            pl.BlockSpec((tm, d), lambda i, j: (i, 0)),
            pl.BlockSpec((None, 1, d), lambda i, j: (row(i), 0, shift_blk)),
            pl.BlockSpec((None, 1, d), lambda i, j: (row(i), 0, scale_blk)),
            pl.BlockSpec((None, d, tn), lambda i, j: (w_layer, 0, j)),
        ],
        out_specs=pl.BlockSpec((tm, tn), lambda i, j: (i, j)),
        out_shape=jax.ShapeDtypeStruct((m, n), BF16),
        scratch_shapes=[pltpu.VMEM((tm, d), BF16)],
        compiler_params=_params("parallel", "arbitrary"),
        name="ln_matmul",
    )(x, mod3, mod3, w)


def _na_plan(n_rows):
    wr = min(WIN_R, n_rows)
    n_qblk = n_rows // NA_Q_ROWS
    r = np.arange(n_rows)
    row_start = np.clip(r - wr // 2, 0, n_rows - wr)
    starts, pat_ids, patterns, keys = [], [], [], {}
    for qi in range(n_qblk):
        k0 = int(np.clip(qi * NA_Q_ROWS - NA_Q_ROWS, 0, n_rows - NA_K_ROWS))
        q_rows = qi * NA_Q_ROWS + np.arange(NA_Q_ROWS)
        k_rows = k0 + np.arange(NA_K_ROWS)
        valid = (k_rows[None, :] >= row_start[q_rows][:, None]) & (k_rows[None, :] < row_start[q_rows][:, None] + wr)
        assert valid.sum(axis=1).min() == wr, "key window must cover every query row's neighbourhood"
        dr_idx = np.clip(k_rows[None, :] - q_rows[:, None] + WIN_R - 1, 0, 2 * WIN_R - 2)
        key = (valid.tobytes(), dr_idx.tobytes())
        if key not in keys:
            keys[key] = len(patterns)
            patterns.append((valid, dr_idx))
        starts.append(k0)
        pat_ids.append(keys[key])
    return starts, pat_ids, patterns


def _na_bias(rpb, patterns):
    n, n_dr, n_dc = rpb.shape
    col = np.arange(GRID_W)
    col_start = np.clip(col - WIN_C // 2, 0, GRID_W - WIN_C)
    in_win = (col[None, :] >= col_start[:, None]) & (col[None, :] < col_start[:, None] + WIN_C)
    assert np.abs(col[None, :] - col[:, None])[in_win].max() <= WIN_C - 1
    line = 2 * GRID_W
    left = GRID_W - 1 - (WIN_C - 1)
    w = jnp.pad(rpb.astype(F32), ((0, 0), (0, 0), (left, line - left - n_dc)))
    w = jnp.broadcast_to(w[:, :, None, :], (n, n_dr, GRID_W, line)).reshape(n, n_dr, GRID_W * line)
    toe = w[:, :, :GRID_W * (line - 1)].reshape(n, n_dr, GRID_W, line - 1)[..., GRID_W - 1:]
    toe = jnp.where(in_win[None, None], toe * LOG2E, NEG_INF).transpose(0, 2, 1, 3)
    out = []
    for valid, dr_idx in patterns:
        per_row = []
        for qr in range(NA_Q_ROWS):
            kr = np.flatnonzero(valid[qr])
            a, b = int(kr[0]), int(kr[-1]) + 1
            d0 = int(dr_idx[qr, a])
            assert b - a == len(kr) and (dr_idx[qr, a:b] == d0 + np.arange(b - a)).all()
            piece = toe[:, :, d0:d0 + (b - a), :]
            per_row.append(jnp.pad(piece, ((0, 0), (0, 0), (a, NA_K_ROWS - b), (0, 0)), constant_values=NEG_INF))
        out.append(jnp.stack(per_row, axis=1).reshape(n, NA_QBLK, NA_KWIN))
    return jnp.stack(out, axis=1)


def _fill_values_with_ones(vs_ref, v_ref, vc_ref, seq):
    vs_ref[0:seq, 0:HEAD_DIM] = v_ref[...]
    vs_ref[seq:, 0:HEAD_DIM] = vc_ref[...]
    vs_ref[:, HEAD_DIM:] = jnp.ones((vs_ref.shape[0], HEAD_DIM), vs_ref.dtype)


def _normalised(o_aug):
    return o_aug[:, :HEAD_DIM] / o_aug[:, HEAD_DIM:]


def _na_kernel(q_ref, k_ref, v_ref, kc_ref, vc_ref, bias_ref, o_ref, vs_ref, *, starts, pat_ids, seq):
    _fill_values_with_ones(vs_ref, v_ref, vc_ref, seq)
    kc = kc_ref[...]
    for qi, (k0, pat) in enumerate(zip(starts, pat_ids)):
        q = q_ref[qi * NA_QBLK:(qi + 1) * NA_QBLK, :]
        kw = k_ref[k0 * GRID_W:k0 * GRID_W + NA_KWIN, :]
        s_loc = _dot_nt(q, kw) * SCORE_SCALE + bias_ref[pat]
        s_ctx = _dot_nt(q, kc) * SCORE_SCALE
        m = jnp.maximum(jnp.max(s_loc, axis=-1, keepdims=True), jnp.max(s_ctx, axis=-1, keepdims=True))
        p_loc = jnp.exp2(s_loc - m).astype(BF16)
        p_ctx = jnp.exp2(s_ctx - m).astype(BF16)
        o_aug = _dot(p_loc, vs_ref[k0 * GRID_W:k0 * GRID_W + NA_KWIN, :]) + _dot(p_ctx, vs_ref[seq:, :])
        o_ref[qi * NA_QBLK:(qi + 1) * NA_QBLK, :] = _normalised(o_aug).astype(o_ref.dtype)


def _na_attention(z, zc, kc_col, vc_col, bias, layer, plan, batch, seq, ctx_len):
    starts, pat_ids, patterns = plan
    hb = HEAD_DIM
    kernel = functools.partial(_na_kernel, starts=tuple(starts), pat_ids=tuple(pat_ids), seq=seq)
    return pl.pallas_call(
        kernel,
        grid=(NA_HEADS, batch),
        in_specs=[
            pl.BlockSpec((seq, hb), lambda h, b: (b, QA0 // hb + h)),
            pl.BlockSpec((seq, hb), lambda h, b: (b, KA0 // hb + h)),
            pl.BlockSpec((seq, hb), lambda h, b: (b, VA0 // hb + h)),
            pl.BlockSpec((ctx_len, hb), lambda h, b: (b, kc_col // hb + h)),
            pl.BlockSpec((ctx_len, hb), lambda h, b: (b, vc_col // hb + h)),
            pl.BlockSpec((None, len(patterns), NA_QBLK, NA_KWIN), lambda h, b: (layer * NA_HEADS + h, 0, 0, 0)),
        ],
        out_specs=pl.BlockSpec((seq, hb), lambda h, b: (b, h)),
        out_shape=jax.ShapeDtypeStruct((batch * seq, BRANCH_W), BF16),
        scratch_shapes=[pltpu.VMEM((seq + ctx_len, 2 * hb), BF16)],
        compiler_params=_params("parallel", "parallel"),
        name="na_attention",
    )(z, z, z, zc, zc, bias)


def _gqa_kernel(q_ref, k_ref, v_ref, kc_ref, vc_ref, qg_ref, kg_ref, cos_ref, sin_ref, o_ref,
                ks_ref, vs_ref, *, tq, seq, qrows):
    qi = pl.program_id(2)

    @pl.when(qi == 0)
    def _():
        kg = kg_ref[...]
        kn = _rms_norm(k_ref[...].astype(F32), kg)
        ks_ref[0:seq, :] = _rope(kn, cos_ref[...], sin_ref[...]).astype(BF16)
        ks_ref[seq:, :] = _rms_norm(kc_ref[...].astype(F32), kg).astype(BF16)
        _fill_values_with_ones(vs_ref, v_ref, vc_ref, seq)

    qg = qg_ref[...]
    for c in range(tq // qrows):
        r = pl.multiple_of(qi * tq + c * qrows, qrows)
        cos = cos_ref[pl.ds(r, qrows), :]
        sin = sin_ref[pl.ds(r, qrows), :]
        rows = slice(c * qrows, (c + 1) * qrows)
        for g in range(GQA_GROUP):
            cols = slice(g * HEAD_DIM, (g + 1) * HEAD_DIM)
            q = q_ref[rows, cols].astype(F32)
            qn = (_rope(_rms_norm(q, qg), cos, sin) * SCORE_SCALE).astype(BF16)
            s = _dot_nt(qn, ks_ref[...])
            m = jnp.max(s, axis=-1, keepdims=True)
            p = jnp.exp2(s - m).astype(BF16)
            o_ref[rows, cols] = _normalised(_dot(p, vs_ref[...])).astype(o_ref.dtype)


def _gqa_attention(z, zc, kc_col, vc_col, q_gain, k_gain, layer, cos, sin, batch, seq, ctx_len, tq=2048,
                   qrows=256):
    hb = HEAD_DIM
    gw = GQA_GROUP * HEAD_DIM
    nq = seq // tq
    kernel = functools.partial(_gqa_kernel, tq=tq, seq=seq, qrows=qrows)
    gain = pl.BlockSpec((None, 1, hb), lambda b, kh, qi: (layer, 0, 0))
    return pl.pallas_call(
        kernel,
        grid=(batch, GQA_KV_HEADS, nq),
        in_specs=[
            pl.BlockSpec((tq, gw), lambda b, kh, qi: (b * nq + qi, QB0 // gw + kh)),
            pl.BlockSpec((seq, hb), lambda b, kh, qi: (b, KB0 // hb + kh)),
            pl.BlockSpec((seq, hb), lambda b, kh, qi: (b, VB0 // hb + kh)),
            pl.BlockSpec((ctx_len, hb), lambda b, kh, qi: (b, kc_col // hb + kh)),
            pl.BlockSpec((ctx_len, hb), lambda b, kh, qi: (b, vc_col // hb + kh)),
            gain, gain,
            pl.BlockSpec((seq, hb), lambda b, kh, qi: (0, 0)),
            pl.BlockSpec((seq, hb), lambda b, kh, qi: (0, 0)),
        ],
        out_specs=pl.BlockSpec((tq, gw), lambda b, kh, qi: (b * nq + qi, kh)),
        out_shape=jax.ShapeDtypeStruct((batch * seq, BRANCH_W), BF16),
        scratch_shapes=[pltpu.VMEM((seq + ctx_len, hb), BF16), pltpu.VMEM((seq + ctx_len, 2 * hb), BF16)],
        compiler_params=_params("parallel", "parallel", "arbitrary"),
        name="gqa_attention",
    )(z, z, z, zc, zc, q_gain, k_gain, cos, sin)


def _ctx_attn_kernel(q_ref, k_ref, v_ref, qg_ref, kg_ref, o_ref, *, qk_norm):
    q = q_ref[...]
    k = k_ref[...]
    if qk_norm:
        q = _rms_norm(q.astype(F32), qg_ref[...]).astype(BF16)
        k = _rms_norm(k.astype(F32), kg_ref[...]).astype(BF16)
    s = _dot_nt(q, k) * ATTN_SCALE
    m = jnp.max(s, axis=-1, keepdims=True)
    p = jnp.exp(s - m)
    den = jnp.sum(p, axis=-1, keepdims=True)
    o_ref[...] = (_dot(p.astype(BF16), v_ref[...]) / den).astype(o_ref.dtype)


def _ctx_attention(zc, q_col, k_col, v_col, kv_group, q_gain, k_gain, layer, qk_norm, batch, ctx_len):
    hb = HEAD_DIM
    kernel = functools.partial(_ctx_attn_kernel, qk_norm=qk_norm)
    gain = pl.BlockSpec((None, 1, hb), lambda b, h: (layer, 0, 0))
    return pl.pallas_call(
        kernel,
        grid=(batch, BRANCH_W // hb),
        in_specs=[
            pl.BlockSpec((ctx_len, hb), lambda b, h: (b, q_col // hb + h)),
            pl.BlockSpec((ctx_len, hb), lambda b, h: (b, k_col // hb + h // kv_group)),
            pl.BlockSpec((ctx_len, hb), lambda b, h: (b, v_col // hb + h // kv_group)),
            gain, gain,
        ],
        out_specs=pl.BlockSpec((ctx_len, hb), lambda b, h: (b, h)),
        out_shape=jax.ShapeDtypeStruct((batch * ctx_len, BRANCH_W), BF16),
        compiler_params=_params("parallel", "parallel"),
        name="ctx_attention",
    )(zc, zc, zc, q_gain, k_gain)


def _conv_kernel(b_ref, c_ref, u_ref, w_ref, o_ref):
    v = c_ref[...].astype(F32) * u_ref[...].astype(F32)
    n = v.shape[0]
    row = lax.broadcasted_iota(jnp.int32, v.shape, 0)
    prev = jnp.where(row == 0, 0.0, pltpu.roll(v, 1, axis=0))
    nxt = jnp.where(row == n - 1, 0.0, pltpu.roll(v, n - 1, axis=0))
    w = w_ref[...]
    y = w[0:1, :] * prev + w[1:2, :] * v + w[2:3, :] * nxt
    o_ref[...] = (b_ref[...].astype(F32) * y).astype(o_ref.dtype)


def _short_conv(z, conv_w, layer, n_seq, seq_len, tc=256):
    return pl.pallas_call(
        _conv_kernel,
        grid=(n_seq, BRANCH_W // tc),
        in_specs=[
            pl.BlockSpec((seq_len, tc), lambda s, j: (s, CB0 // tc + j)),
            pl.BlockSpec((seq_len, tc), lambda s, j: (s, CC0 // tc + j)),
            pl.BlockSpec((seq_len, tc), lambda s, j: (s, CX0 // tc + j)),
            pl.BlockSpec((None, CONV_K, tc), lambda s, j: (layer, 0, j)),
        ],
        out_specs=pl.BlockSpec((seq_len, tc), lambda s, j: (s, j)),
        out_shape=jax.ShapeDtypeStruct((n_seq * seq_len, BRANCH_W), BF16),
        compiler_params=_params("parallel", "parallel"),
        name="short_conv",
    )(z, z, z, conv_w)


def _merge_kernel(a0_ref, a1_ref, a2_ref, g_ref, w_ref, o_ref, *, tn):
    d = o_ref.shape[1]
    for j in range(d // tn):
        cols = slice(j * tn, (j + 1) * tn)
        acc = None
        for k, a_ref in enumerate((a0_ref, a1_ref, a2_ref)):
            gate = jax.nn.sigmoid(g_ref[:, k * d + j * tn:k * d + (j + 1) * tn].astype(F32))
            term = gate * _dot(a_ref[...], w_ref[k, :, cols])
            acc = term if acc is None else acc + term
        o_ref[:, cols] = acc.astype(o_ref.dtype)


def _merge(o_na, o_gqa, y_conv, z, w_branch, layer, tm=512, tn=512):
    m = o_na.shape[0]
    gw = N_BRANCH * D_MODEL
    assert G0 % gw == 0
    act = pl.BlockSpec((tm, BRANCH_W), lambda i: (i, 0))
    return pl.pallas_call(
        functools.partial(_merge_kernel, tn=tn),
        grid=(m // tm,),
        in_specs=[act, act, act,
                  pl.BlockSpec((tm, gw), lambda i: (i, G0 // gw)),
                  _resident((None, N_BRANCH, BRANCH_W, D_MODEL), lambda i: (layer, 0, 0, 0))],
        out_specs=pl.BlockSpec((tm, D_MODEL), lambda i: (i, 0)),
        out_shape=jax.ShapeDtypeStruct((m, D_MODEL), BF16),
        compiler_params=_params("parallel"),
        name="merge",
    )(o_na, o_gqa, y_conv, z, w_branch)


def _out_proj_kernel(m_ref, w_ref, x_ref, gate_ref, g_ref, b_ref, o_ref, *, alpha, rows):
    gate = gate_ref[...]
    g = g_ref[...]
    b = b_ref[...]
    for c in range(o_ref.shape[0] // rows):
        f = _dot(m_ref[c * rows:(c + 1) * rows, :], w_ref[...])
        _residual_norm_rows(x_ref, f, gate, g, b, o_ref, c * rows, alpha)


def _out_proj(m_act, w_o, x, mod3, layer, gate_blk, ln_g, ln_b, ln_row, alpha, rows_per_mod, fixed_row,
              tm=512, rows=256):
    m, d = x.shape
    row = _mod_row_map(layer, rows_per_mod, tm, fixed_row)
    vec = pl.BlockSpec((None, 1, d), lambda i: (ln_row, 0, 0))
    return pl.pallas_call(
        functools.partial(_out_proj_kernel, alpha=alpha, rows=rows),
        grid=(m // tm,),
        in_specs=[
            pl.BlockSpec((tm, d), lambda i: (i, 0)),
            _resident((None, d, d), lambda i: (layer, 0, 0)),
            pl.BlockSpec((tm, d), lambda i: (i, 0)),
            pl.BlockSpec((None, 1, d), lambda i: (row(i), 0, gate_blk)),
            vec, vec,
        ],
        out_specs=pl.BlockSpec((tm, d), lambda i: (i, 0)),
        out_shape=jax.ShapeDtypeStruct((m, d), F32),
        compiler_params=_params("parallel"),
        name="out_proj",
    )(m_act, w_o, x, mod3, ln_g, ln_b)


def _mlp_kernel(x_ref, sh_ref, sc_ref, gate_ref, g_ref, b_ref, wu_ref, wd_ref, o_ref, h_ref, *, alpha, rows):
    j = pl.program_id(1)
    last = pl.num_programs(1) - 1
    n_chunks = o_ref.shape[0] // rows

    def down(r):
        u = jnp.maximum(_dot(h_ref[r, :], wu_ref[...]), 0.0)
        return _dot((u * u).astype(BF16), wd_ref[...])

    @pl.when(j == 0)
    def _():
        sc1 = 1.0 + sc_ref[...]
        sh = sh_ref[...]
        for c in range(n_chunks):
            _modulate_rows(x_ref, sh, sc1, h_ref, c * rows, rows)
            r = slice(c * rows, (c + 1) * rows)
            o_ref[r, :] = down(r)

    @pl.when(jnp.logical_and(j > 0, j < last))
    def _():
        o_ref[...] += down(slice(None))

    @pl.when(j == last)
    def _():
        gate = gate_ref[...]
        g = g_ref[...]
        b = b_ref[...]
        for c in range(n_chunks):
            r = slice(c * rows, (c + 1) * rows)
            _residual_norm_rows(x_ref, o_ref[r, :] + down(r), gate, g, b, o_ref, c * rows, alpha)


def _mlp(x, mod3, layer, shift_blk, scale_blk, gate_blk, ln_g, ln_b, ln_row, w_up, w_down, alpha,
         rows_per_mod, fixed_row, tm=512, th=1024, rows=256):
    m, d = x.shape
    hid = w_up.shape[2]
    assert hid // th >= 2
    row = _mod_row_map(layer, rows_per_mod, tm, fixed_row)
    mod_spec = lambda blk: pl.BlockSpec((None, 1, d), lambda i, j: (row(i), 0, blk))
    vec = pl.BlockSpec((None, 1, d), lambda i, j: (ln_row, 0, 0))
    return pl.pallas_call(
        functools.partial(_mlp_kernel, alpha=alpha, rows=rows),
        grid=(m // tm, hid // th),
        in_specs=[
            pl.BlockSpec((tm, d), lambda i, j: (i, 0)),
            mod_spec(shift_blk), mod_spec(scale_blk), mod_spec(gate_blk),
            vec, vec,
            pl.BlockSpec((None, d, th), lambda i, j: (layer, 0, j)),
            pl.BlockSpec((None, th, d), lambda i, j: (layer, j, 0)),
        ],
        out_specs=pl.BlockSpec((tm, d), lambda i, j: (i, 0)),
        out_shape=jax.ShapeDtypeStruct((m, d), F32),
        scratch_shapes=[pltpu.VMEM((tm, d), BF16)],
        compiler_params=_params("parallel", "arbitrary"),
        name="mlp",
    )(x, mod3, mod3, mod3, ln_g, ln_b, w_up, w_down)


def _rope_tables(seq):
    t = jnp.arange(seq)
    row = (t // GRID_W).astype(F32)
    col = (t % GRID_W).astype(F32)
    axis_dim = HEAD_DIM // 2
    freqs = ROPE_THETA ** (-jnp.arange(0, axis_dim, 2, dtype=F32) / axis_dim)
    ar, ac = row[:, None] * freqs, col[:, None] * freqs
    cos = jnp.concatenate([jnp.cos(ar), jnp.cos(ar), jnp.cos(ac), jnp.cos(ac)], axis=-1)
    sin = jnp.concatenate([-jnp.sin(ar), jnp.sin(ar), -jnp.sin(ac), jnp.sin(ac)], axis=-1)
    return cos, sin


def kernel(x, c, ctx, c_ctx, w_mod, b_mod, w_in, rpb, q_gain, k_gain, conv_w, w_branch, w_o, w_up, w_down,
           ln_g, ln_b):
    batch, seq, d = x.shape
    ctx_len = ctx.shape[1]
    depth = w_mod.shape[0]
    assert d == D_MODEL and batch < MOD_ROWS and seq % NA_QBLK == 0 and seq // GRID_W >= NA_K_ROWS
    alpha = float((2 * depth) ** 0.25)
    ctx_row = batch

    c_all = jnp.concatenate([c, c_ctx[None, :], jnp.zeros((MOD_ROWS - batch - 1, d), F32)], axis=0)
    mod3 = _modulation(c_all, w_mod, b_mod).reshape(depth * MOD_ROWS, 1, 6 * d)
    cos, sin = _rope_tables(seq)
    plan = _na_plan(seq // GRID_W)
    bias = _na_bias(rpb.reshape((depth * NA_HEADS,) + rpb.shape[2:]), plan[2])

    w_in_b = _w_in_columns(w_in, _Z_ORDER).astype(BF16)
    w_kv_b = _w_in_columns(w_in[depth - 1:], ("ka", "va", "kb", "vb")).astype(BF16)
    wb_b, wo_b, wu_b, wd_b = (w.astype(BF16) for w in (w_branch, w_o, w_up, w_down))
    qg = q_gain.reshape(depth, 1, HEAD_DIM)
    kg = k_gain.reshape(depth, 1, HEAD_DIM)
    lg = ln_g.reshape(depth * 2, 1, d)
    lb = ln_b.reshape(depth * 2, 1, d)

    xl = x.reshape(batch * seq, d)
    xc = ctx.reshape(batch * ctx_len, d)
    for l in range(depth):
        with_ctx_out = l < depth - 1

        z = _ln_matmul(xl, mod3, l, 0, 1, w_in_b, l, seq, None, tm=1024, tn=1536)
        if with_ctx_out:
            zc = _ln_matmul(xc, mod3, l, 0, 1, w_in_b, l, None, ctx_row, tm=512, tn=1536)
            ka_c, va_c, kb_c, vb_c = KA0, VA0, KB0, VB0
        else:
            zc = _ln_matmul(xc, mod3, l, 0, 1, w_kv_b, 0, None, ctx_row, tm=512, tn=w_kv_b.shape[2] // 2)
            ka_c, va_c, kb_c, vb_c = 0, BRANCH_W, 2 * BRANCH_W, 2 * BRANCH_W + KV_W

        o_na = _na_attention(z, zc, ka_c, va_c, bias, l, plan, batch, seq, ctx_len)
        o_gqa = _gqa_attention(z, zc, kb_c, vb_c, qg, kg, l, cos, sin, batch, seq, ctx_len)
        y_conv = _short_conv(z, conv_w, l, batch, seq)
        m_act = _merge(o_na, o_gqa, y_conv, z, wb_b, l)
        xl = _out_proj(m_act, wo_b, xl, mod3, l, 2, lg, lb, 2 * l, alpha, seq, None)
        xl = _mlp(xl, mod3, l, 3, 4, 5, lg, lb, 2 * l + 1, wu_b, wd_b, alpha, seq, None)

        if with_ctx_out:
            o_na_c = _ctx_attention(zc, QA0, KA0, VA0, 1, qg, kg, l, False, batch, ctx_len)
            o_gqa_c = _ctx_attention(zc, QB0, KB0, VB0, GQA_GROUP, qg, kg, l, True, batch, ctx_len)
            y_conv_c = _short_conv(zc, conv_w, l, batch, ctx_len)
            m_c = _merge(o_na_c, o_gqa_c, y_conv_c, zc, wb_b, l)
            xc = _out_proj(m_c, wo_b, xc, mod3, l, 2, lg, lb, 2 * l, alpha, None, ctx_row)
            xc = _mlp(xc, mod3, l, 3, 4, 5, lg, lb, 2 * l + 1, wu_b, wd_b, alpha, None, ctx_row)
    return xl.reshape(batch, seq, d)
```

```python
import functools

import numpy as np
import jax
import jax.numpy as jnp
from jax import lax
from jax.experimental import pallas as pl
from jax.experimental.pallas import tpu as pltpu

D_MODEL = 2048
GRID_W = 64
HEAD_DIM = 128
BRANCH_W = D_MODEL // 2
NA_HEADS = BRANCH_W // HEAD_DIM
GQA_HEADS = BRANCH_W // HEAD_DIM
GQA_KV_HEADS = GQA_HEADS // 4
GQA_GROUP = GQA_HEADS // GQA_KV_HEADS
KV_W = GQA_KV_HEADS * HEAD_DIM
CONV_K = 3
MLP_HIDDEN = 4 * D_MODEL
N_BRANCH = 3
WIN_R = 8
WIN_C = 16
ROPE_THETA = 10000.0
NORM_EPS = 1e-6
NEG_INF = -1e30
ATTN_SCALE = HEAD_DIM ** -0.5
LOG2E = 1.4426950408889634
SCORE_SCALE = ATTN_SCALE * LOG2E

W_QA = 0
W_QB = W_QA + BRANCH_W
W_KA = W_QB + BRANCH_W
W_VA = W_KA + BRANCH_W
W_KB = W_VA + BRANCH_W
W_VB = W_KB + KV_W
W_CB = W_VB + KV_W
W_CC = W_CB + BRANCH_W
W_CX = W_CC + BRANCH_W
W_G = W_CX + BRANCH_W
IN_W = W_G + N_BRANCH * D_MODEL
CONV_W = 3 * BRANCH_W
GATE_W = N_BRANCH * D_MODEL
Z_ROT = 1536


class _Cols:
    def __init__(self, start, width, rot):
        self.start, self.width, self.rot = start, width, rot

    def block(self, w_col, blk):
        off = (w_col - self.start - self.rot) % self.width
        assert off % blk == 0 and self.width % blk == 0
        n, base = self.width // blk, off // blk
        return lambda k: (base + k) % n

    def index(self, w_col, blk):
        off = (w_col - self.start - self.rot) % self.width
        assert off % blk == 0 and off + blk <= self.width
        return off // blk


Z_COLS = _Cols(0, IN_W, Z_ROT)
KV_COLS = _Cols(W_KA, W_CB - W_KA, 0)

MOD_ROWS = 32
NA_Q_ROWS = 4
NA_K_ROWS = 12
NA_QBLK = NA_Q_ROWS * GRID_W
NA_KWIN = NA_K_ROWS * GRID_W
SUBLANES = 8

VMEM_LIMIT_BYTES = 56 * 1024 * 1024
LN_CHUNK = 128

F32 = jnp.float32
BF16 = jnp.bfloat16


def _params(*semantics):
    return pltpu.CompilerParams(dimension_semantics=semantics, vmem_limit_bytes=VMEM_LIMIT_BYTES)


def _resident(block_shape, index_map):
    return pl.BlockSpec(block_shape, index_map, pipeline_mode=pl.Buffered(1))


def _layer_norm(x):
    mu = jnp.mean(x, axis=-1, keepdims=True)
    xc = x - mu
    var = jnp.mean(xc * xc, axis=-1, keepdims=True)
    return xc * lax.rsqrt(var + NORM_EPS)


def _rms_norm(x, gain):
    return x * lax.rsqrt(jnp.mean(x * x, axis=-1, keepdims=True) + NORM_EPS) * gain


def _rope(x, cos, sin_signed):
    lane = lax.broadcasted_iota(jnp.int32, x.shape, 1)
    partner = jnp.where((lane & 32) == 0, pltpu.roll(x, 96, axis=1), pltpu.roll(x, 32, axis=1))
    return x * cos + partner * sin_signed


def _dot(a, b):
    return jnp.dot(a, b, preferred_element_type=F32)


def _dot_nt(a, b):
    return lax.dot_general(a, b, (((1,), (1,)), ((), ())), preferred_element_type=F32)


def _modulate_rows(x_ref, sh, sc1, h_ref, r0, rows):
    for s in range(rows // LN_CHUNK):
        rr = slice(r0 + s * LN_CHUNK, r0 + (s + 1) * LN_CHUNK)
        h_ref[rr, :] = (_layer_norm(x_ref[rr, :]) * sc1 + sh).astype(h_ref.dtype)


def _residual_norm_rows(x_ref, f, gate, g, b, o_ref, r0, alpha):
    for s in range(f.shape[0] // LN_CHUNK):
        rr = slice(r0 + s * LN_CHUNK, r0 + (s + 1) * LN_CHUNK)
        y = alpha * x_ref[rr, :] + gate * f[s * LN_CHUNK:(s + 1) * LN_CHUNK, :]
        o_ref[rr, :] = _layer_norm(y) * g + b


def _mod_row_map(layer, rows_per_mod, tm, fixed_row):
    base = layer * MOD_ROWS
    if fixed_row is not None:
        return lambda i: base + fixed_row
    return lambda i: base + (i * tm) // rows_per_mod


def _mod_kernel(c_ref, w_ref, b_ref, o_ref):
    c = c_ref[...]
    a = (c * jax.nn.sigmoid(c)).astype(BF16)
    o_ref[...] = _dot(a, w_ref[...].astype(BF16)) + b_ref[...]


def _modulation(c_all, w_mod, b_mod, tn=1024):
    depth, d, n = w_mod.shape
    return pl.pallas_call(
        _mod_kernel,
        grid=(depth, n // tn),
        in_specs=[
            pl.BlockSpec((MOD_ROWS, d), lambda l, j: (0, 0)),
            pl.BlockSpec((None, d, tn), lambda l, j: (l, 0, j)),
            pl.BlockSpec((None, 1, tn), lambda l, j: (l, 0, j)),
        ],
        out_specs=pl.BlockSpec((None, MOD_ROWS, tn), lambda l, j: (l, 0, j)),
        out_shape=jax.ShapeDtypeStruct((depth, MOD_ROWS, n), F32),
        compiler_params=_params("parallel", "parallel"),
        name="modulation",
    )(c_all, w_mod, b_mod.reshape(depth, 1, n))


def _ln_matmul_kernel(x_ref, sh_ref, sc_ref, w_ref, o_ref, h_ref, *, rows):
    j = pl.program_id(1)

    @pl.when(j == 0)
    def _():
        sc1 = 1.0 + sc_ref[...]
        sh = sh_ref[...]
        for c in range(x_ref.shape[0] // rows):
            _modulate_rows(x_ref, sh, sc1, h_ref, c * rows, rows)
            r = slice(c * rows, (c + 1) * rows)
            o_ref[r, :] = _dot(h_ref[r, :], w_ref[...]).astype(o_ref.dtype)

    @pl.when(j > 0)
    def _():
        o_ref[...] = _dot(h_ref[...], w_ref[...]).astype(o_ref.dtype)


def _ln_matmul(x, mod3, layer, shift_blk, scale_blk, w, first_col, n, rows_per_mod, fixed_row, tm, tn, rows=256):
    m, d = x.shape
    w_tiles = w.shape[2] // tn
    assert first_col % tn == 0 and n % tn == 0 and w.shape[2] % tn == 0
    row = _mod_row_map(layer, rows_per_mod, tm, fixed_row)
    return pl.pallas_call(
        functools.partial(_ln_matmul_kernel, rows=rows),
        grid=(m // tm, n // tn),
        in_specs=[
            pl.BlockSpec((tm, d), lambda i, j: (i, 0)),
            pl.BlockSpec((None, 1, d), lambda i, j: (row(i), 0, shift_blk)),
            pl.BlockSpec((None, 1, d), lambda i, j: (row(i), 0, scale_blk)),
            pl.BlockSpec((None, d, tn), lambda i, j: (layer, 0, (first_col // tn + j) % w_tiles)),
        ],
        out_specs=pl.BlockSpec((tm, tn), lambda i, j: (i, j)),
        out_shape=jax.ShapeDtypeStruct((m, n), BF16),
        scratch_shapes=[pltpu.VMEM((tm, d), BF16)],
        compiler_params=_params("parallel", "arbitrary"),
        name="ln_matmul",
    )(x, mod3, mod3, w)


def _na_plan(n_rows):
    wr = min(WIN_R, n_rows)
    n_qblk = n_rows // NA_Q_ROWS
    r = np.arange(n_rows)
    row_start = np.clip(r - wr // 2, 0, n_rows - wr)
    starts, pat_ids, patterns, keys = [], [], [], {}
    for qi in range(n_qblk):
        k0 = int(np.clip(qi * NA_Q_ROWS - NA_Q_ROWS, 0, n_rows - NA_K_ROWS))
        q_rows = qi * NA_Q_ROWS + np.arange(NA_Q_ROWS)
        k_rows = k0 + np.arange(NA_K_ROWS)
        valid = (k_rows[None, :] >= row_start[q_rows][:, None]) & (k_rows[None, :] < row_start[q_rows][:, None] + wr)
        assert valid.sum(axis=1).min() == wr, "key window must cover every query row's neighbourhood"
        dr_idx = np.clip(k_rows[None, :] - q_rows[:, None] + WIN_R - 1, 0, 2 * WIN_R - 2)
        key = (valid.tobytes(), dr_idx.tobytes())
        if key not in keys:
            keys[key] = len(patterns)
            patterns.append((valid, dr_idx))
        starts.append(k0)
        pat_ids.append(keys[key])
    return starts, pat_ids, patterns


def _na_bias(rpb, patterns):
    n, n_dr, n_dc = rpb.shape
    col = np.arange(GRID_W)
    col_start = np.clip(col - WIN_C // 2, 0, GRID_W - WIN_C)
    in_win = (col[None, :] >= col_start[:, None]) & (col[None, :] < col_start[:, None] + WIN_C)
    assert np.abs(col[None, :] - col[:, None])[in_win].max() <= WIN_C - 1
    line = 2 * GRID_W
    left = GRID_W - 1 - (WIN_C - 1)
    w = jnp.pad(rpb.astype(F32), ((0, 0), (0, 0), (left, line - left - n_dc)))
    w = jnp.broadcast_to(w[:, :, None, :], (n, n_dr, GRID_W, line)).reshape(n, n_dr, GRID_W * line)
    toe = w[:, :, :GRID_W * (line - 1)].reshape(n, n_dr, GRID_W, line - 1)[..., GRID_W - 1:]
    toe = jnp.where(in_win[None, None], toe * LOG2E, NEG_INF).transpose(0, 2, 1, 3)
    out = []
    for valid, dr_idx in patterns:
        per_row = []
        for qr in range(NA_Q_ROWS):
            kr = np.flatnonzero(valid[qr])
            a, b = int(kr[0]), int(kr[-1]) + 1
            d0 = int(dr_idx[qr, a])
            assert b - a == len(kr) and (dr_idx[qr, a:b] == d0 + np.arange(b - a)).all()
            piece = toe[:, :, d0:d0 + (b - a), :]
            per_row.append(jnp.pad(piece, ((0, 0), (0, 0), (a, NA_K_ROWS - b), (0, 0)), constant_values=NEG_INF))
        out.append(jnp.stack(per_row, axis=1).reshape(n, NA_QBLK, NA_KWIN))
    return jnp.stack(out, axis=1)


def _fill_values_with_ones(vs_ref, v_ref, vc_ref, seq):
    vs_ref[0:seq, 0:HEAD_DIM] = v_ref[...]
    vs_ref[seq:, 0:HEAD_DIM] = vc_ref[...]
    vs_ref[:, HEAD_DIM:] = jnp.ones((vs_ref.shape[0], HEAD_DIM), vs_ref.dtype)


def _normalised(o_aug):
    return o_aug[:, :HEAD_DIM] / o_aug[:, HEAD_DIM:]


def _na_kernel(q_ref, k_ref, v_ref, kc_ref, vc_ref, bias_ref, o_ref, vs_ref, *, starts, pat_ids, seq):
    _fill_values_with_ones(vs_ref, v_ref, vc_ref, seq)
    kc = kc_ref[...]
    for qi, (k0, pat) in enumerate(zip(starts, pat_ids)):
        q = q_ref[qi * NA_QBLK:(qi + 1) * NA_QBLK, :]
        kw = k_ref[k0 * GRID_W:k0 * GRID_W + NA_KWIN, :]
        s_loc = _dot_nt(q, kw) * SCORE_SCALE + bias_ref[pat]
        s_ctx = _dot_nt(q, kc) * SCORE_SCALE
        m = jnp.maximum(jnp.max(s_loc, axis=-1, keepdims=True), jnp.max(s_ctx, axis=-1, keepdims=True))
        p_loc = jnp.exp2(s_loc - m).astype(BF16)
        p_ctx = jnp.exp2(s_ctx - m).astype(BF16)
        o_aug = _dot(p_loc, vs_ref[k0 * GRID_W:k0 * GRID_W + NA_KWIN, :]) + _dot(p_ctx, vs_ref[seq:, :])
        o_ref[qi * NA_QBLK:(qi + 1) * NA_QBLK, :] = _normalised(o_aug).astype(o_ref.dtype)


def _na_attention(z, zc, ctx_cols, bias, layer, plan, batch, seq, ctx_len):
    starts, pat_ids, patterns = plan
    hb = HEAD_DIM
    qa, ka, va = (Z_COLS.block(c, hb) for c in (W_QA, W_KA, W_VA))
    kc, vc = (ctx_cols.block(c, hb) for c in (W_KA, W_VA))
    kernel = functools.partial(_na_kernel, starts=tuple(starts), pat_ids=tuple(pat_ids), seq=seq)
    return pl.pallas_call(
        kernel,
        grid=(NA_HEADS, batch),
        in_specs=[
            pl.BlockSpec((seq, hb), lambda h, b: (b, qa(h))),
            pl.BlockSpec((seq, hb), lambda h, b: (b, ka(h))),
            pl.BlockSpec((seq, hb), lambda h, b: (b, va(h))),
            pl.BlockSpec((ctx_len, hb), lambda h, b: (b, kc(h))),
            pl.BlockSpec((ctx_len, hb), lambda h, b: (b, vc(h))),
            pl.BlockSpec((None, len(patterns), NA_QBLK, NA_KWIN), lambda h, b: (layer * NA_HEADS + h, 0, 0, 0)),
        ],
        out_specs=pl.BlockSpec((seq, hb), lambda h, b: (b, h)),
        out_shape=jax.ShapeDtypeStruct((batch * seq, BRANCH_W), BF16),
        scratch_shapes=[pltpu.VMEM((seq + ctx_len, 2 * hb), BF16)],
        compiler_params=_params("parallel", "parallel"),
        name="na_attention",
    )(z, z, z, zc, zc, bias)


def _gqa_kernel(q_ref, k_ref, v_ref, kc_ref, vc_ref, qg_ref, kg_ref, cos_ref, sin_ref, o_ref,
                ks_ref, vs_ref, *, tq, seq, qrows):
    qi = pl.program_id(2)

    @pl.when(qi == 0)
    def _():
        kg = kg_ref[...]
        kn = _rms_norm(k_ref[...].astype(F32), kg)
        ks_ref[0:seq, :] = _rope(kn, cos_ref[...], sin_ref[...]).astype(BF16)
        ks_ref[seq:, :] = _rms_norm(kc_ref[...].astype(F32), kg).astype(BF16)
        _fill_values_with_ones(vs_ref, v_ref, vc_ref, seq)

    qg = qg_ref[...]
    for c in range(tq // qrows):
        r = pl.multiple_of(qi * tq + c * qrows, qrows)
        cos = cos_ref[pl.ds(r, qrows), :]
        sin = sin_ref[pl.ds(r, qrows), :]
        rows = slice(c * qrows, (c + 1) * qrows)
        for g in range(GQA_GROUP):
            cols = slice(g * HEAD_DIM, (g + 1) * HEAD_DIM)
            q = q_ref[rows, cols].astype(F32)
            qn = (_rope(_rms_norm(q, qg), cos, sin) * SCORE_SCALE).astype(BF16)
            s = _dot_nt(qn, ks_ref[...])
            m = jnp.max(s, axis=-1, keepdims=True)
            p = jnp.exp2(s - m).astype(BF16)
            o_ref[rows, cols] = _normalised(_dot(p, vs_ref[...])).astype(o_ref.dtype)


def _gqa_attention(z, zc, ctx_cols, q_gain, k_gain, layer, cos, sin, batch, seq, ctx_len, tq=None, qrows=256):
    hb = HEAD_DIM
    gw = GQA_GROUP * HEAD_DIM
    tq = seq if tq is None else tq
    nq = seq // tq
    qb = Z_COLS.block(W_QB, gw)
    kb, vb = (Z_COLS.block(c, hb) for c in (W_KB, W_VB))
    kc, vc = (ctx_cols.block(c, hb) for c in (W_KB, W_VB))
    kernel = functools.partial(_gqa_kernel, tq=tq, seq=seq, qrows=qrows)
    gain = pl.BlockSpec((None, 1, hb), lambda b, kh, qi: (layer, 0, 0))
    return pl.pallas_call(
        kernel,
        grid=(batch, GQA_KV_HEADS, nq),
        in_specs=[
            pl.BlockSpec((tq, gw), lambda b, kh, qi: (b * nq + qi, qb(kh))),
            pl.BlockSpec((seq, hb), lambda b, kh, qi: (b, kb(kh))),
            pl.BlockSpec((seq, hb), lambda b, kh, qi: (b, vb(kh))),
            pl.BlockSpec((ctx_len, hb), lambda b, kh, qi: (b, kc(kh))),
            pl.BlockSpec((ctx_len, hb), lambda b, kh, qi: (b, vc(kh))),
            gain, gain,
            pl.BlockSpec((seq, hb), lambda b, kh, qi: (0, 0)),
            pl.BlockSpec((seq, hb), lambda b, kh, qi: (0, 0)),
        ],
        out_specs=pl.BlockSpec((tq, gw), lambda b, kh, qi: (b * nq + qi, kh)),
        out_shape=jax.ShapeDtypeStruct((batch * seq, BRANCH_W), BF16),
        scratch_shapes=[pltpu.VMEM((seq + ctx_len, hb), BF16), pltpu.VMEM((seq + ctx_len, 2 * hb), BF16)],
        compiler_params=_params("parallel", "parallel", "arbitrary"),
        name="gqa_attention",
    )(z, z, z, zc, zc, q_gain, k_gain, cos, sin)


def _ctx_attn_kernel(q_ref, k_ref, v_ref, qg_ref, kg_ref, o_ref, *, qk_norm):
    q = q_ref[...]
    k = k_ref[...]
    if qk_norm:
        q = _rms_norm(q.astype(F32), qg_ref[...]).astype(BF16)
        k = _rms_norm(k.astype(F32), kg_ref[...]).astype(BF16)
    s = _dot_nt(q, k) * ATTN_SCALE
    m = jnp.max(s, axis=-1, keepdims=True)
    p = jnp.exp(s - m)
    den = jnp.sum(p, axis=-1, keepdims=True)
    o_ref[...] = (_dot(p.astype(BF16), v_ref[...]) / den).astype(o_ref.dtype)


def _ctx_attention(zc, w_q, w_k, w_v, kv_group, q_gain, k_gain, layer, qk_norm, batch, ctx_len):
    hb = HEAD_DIM
    qc, kc, vc = (Z_COLS.block(c, hb) for c in (w_q, w_k, w_v))
    kernel = functools.partial(_ctx_attn_kernel, qk_norm=qk_norm)
    gain = pl.BlockSpec((None, 1, hb), lambda b, h: (layer, 0, 0))
    return pl.pallas_call(
        kernel,
        grid=(batch, BRANCH_W // hb),
        in_specs=[
            pl.BlockSpec((ctx_len, hb), lambda b, h: (b, qc(h))),
            pl.BlockSpec((ctx_len, hb), lambda b, h: (b, kc(h // kv_group))),
            pl.BlockSpec((ctx_len, hb), lambda b, h: (b, vc(h // kv_group))),
            gain, gain,
        ],
        out_specs=pl.BlockSpec((ctx_len, hb), lambda b, h: (b, h)),
        out_shape=jax.ShapeDtypeStruct((batch * ctx_len, BRANCH_W), BF16),
        compiler_params=_params("parallel", "parallel"),
        name="ctx_attention",
    )(zc, zc, zc, q_gain, k_gain)


def _conv_into(c_ref, prev_ref, next_ref, w_ref, y_ref, *, seq_len, tc):
    tm = y_ref.shape[0]
    r0 = pl.program_id(0) * tm
    t = (r0 + lax.broadcasted_iota(jnp.int32, (tm, 1), 0)) & (seq_len - 1)
    row = lax.broadcasted_iota(jnp.int32, (tm, 1), 0)
    at_start, at_end = t == 0, t == seq_len - 1
    w = w_ref[...]
    for j in range(BRANCH_W // tc):
        cb = slice(j * tc, (j + 1) * tc)
        cc = slice(BRANCH_W + j * tc, BRANCH_W + (j + 1) * tc)
        cx = slice(2 * BRANCH_W + j * tc, 2 * BRANCH_W + (j + 1) * tc)
        v = c_ref[:, cc].astype(F32) * c_ref[:, cx].astype(F32)
        above = prev_ref[SUBLANES - 1:SUBLANES, cc].astype(F32) * prev_ref[SUBLANES - 1:SUBLANES, cx].astype(F32)
        below = next_ref[0:1, cc].astype(F32) * next_ref[0:1, cx].astype(F32)
        prev = jnp.where(row == 0, above, pltpu.roll(v, 1, axis=0))
        nxt = jnp.where(row == tm - 1, below, pltpu.roll(v, tm - 1, axis=0))
        prev = jnp.where(at_start, 0.0, prev)
        nxt = jnp.where(at_end, 0.0, nxt)
        y = w[0:1, cb] * prev + w[1:2, cb] * v + w[2:3, cb] * nxt
        y_ref[:, cb] = (c_ref[:, cb].astype(F32) * y).astype(y_ref.dtype)


def _merge_kernel(a0_ref, a1_ref, c_ref, prev_ref, next_ref, cw_ref, g_ref, w_ref, o_ref, y_ref, *,
                  tn, seq_len, tc):
    _conv_into(c_ref, prev_ref, next_ref, cw_ref, y_ref, seq_len=seq_len, tc=tc)
    d = o_ref.shape[1]
    for j in range(d // tn):
        cols = slice(j * tn, (j + 1) * tn)
        acc = None
        for k, a_ref in enumerate((a0_ref, a1_ref, y_ref)):
            gate = jax.nn.sigmoid(g_ref[:, k * d + j * tn:k * d + (j + 1) * tn].astype(F32))
            term = gate * _dot(a_ref[...], w_ref[k, :, cols])
            acc = term if acc is None else acc + term
        o_ref[:, cols] = acc.astype(o_ref.dtype)


def _merge(o_na, o_gqa, z, conv_w, w_branch, layer, seq_len, tm=512, tn=512, tc=256):
    m = o_na.shape[0]
    assert seq_len & (seq_len - 1) == 0 and (tm % seq_len == 0 or seq_len % tm == 0) and m % seq_len == 0
    conv_blk, gate_blk = Z_COLS.index(W_CB, CONV_W), Z_COLS.index(W_G, GATE_W)
    halo = tm // SUBLANES
    last_halo = m // SUBLANES - 1
    act = pl.BlockSpec((tm, BRANCH_W), lambda i: (i, 0))
    return pl.pallas_call(
        functools.partial(_merge_kernel, tn=tn, seq_len=seq_len, tc=tc),
        grid=(m // tm,),
        in_specs=[act, act,
                  pl.BlockSpec((tm, CONV_W), lambda i: (i, conv_blk)),
                  pl.BlockSpec((SUBLANES, CONV_W), lambda i: (jnp.maximum(i * halo - 1, 0), conv_blk)),
                  pl.BlockSpec((SUBLANES, CONV_W), lambda i: (jnp.minimum((i + 1) * halo, last_halo), conv_blk)),
                  pl.BlockSpec((None, CONV_K, BRANCH_W), lambda i: (layer, 0, 0)),
                  pl.BlockSpec((tm, GATE_W), lambda i: (i, gate_blk)),
                  _resident((None, N_BRANCH, BRANCH_W, D_MODEL), lambda i: (layer, 0, 0, 0))],
        out_specs=pl.BlockSpec((tm, D_MODEL), lambda i: (i, 0)),
        out_shape=jax.ShapeDtypeStruct((m, D_MODEL), BF16),
        scratch_shapes=[pltpu.VMEM((tm, BRANCH_W), BF16)],
        compiler_params=_params("parallel"),
        name="merge",
    )(o_na, o_gqa, z, z, z, conv_w, z, w_branch)


def _out_proj_kernel(m_ref, w_ref, x_ref, gate_ref, g_ref, b_ref, o_ref, *, alpha, rows):
    gate = gate_ref[...]
    g = g_ref[...]
    b = b_ref[...]
    for c in range(o_ref.shape[0] // rows):
        f = _dot(m_ref[c * rows:(c + 1) * rows, :], w_ref[...])
        _residual_norm_rows(x_ref, f, gate, g, b, o_ref, c * rows, alpha)


def _out_proj(m_act, w_o, x, mod3, layer, gate_blk, ln_g, ln_b, ln_row, alpha, rows_per_mod, fixed_row,
              tm=512, rows=256):
    m, d = x.shape
    row = _mod_row_map(layer, rows_per_mod, tm, fixed_row)
    vec = pl.BlockSpec((None, 1, d), lambda i: (ln_row, 0, 0))
    return pl.pallas_call(
        functools.partial(_out_proj_kernel, alpha=alpha, rows=rows),
        grid=(m // tm,),
        in_specs=[
            pl.BlockSpec((tm, d), lambda i: (i, 0)),
            _resident((None, d, d), lambda i: (layer, 0, 0)),
            pl.BlockSpec((tm, d), lambda i: (i, 0)),
            pl.BlockSpec((None, 1, d), lambda i: (row(i), 0, gate_blk)),
            vec, vec,
        ],
        out_specs=pl.BlockSpec((tm, d), lambda i: (i, 0)),
        out_shape=jax.ShapeDtypeStruct((m, d), F32),
        compiler_params=_params("parallel"),
        name="out_proj",
    )(m_act, w_o, x, mod3, ln_g, ln_b)


def _mlp_kernel(x_ref, sh_ref, sc_ref, gate_ref, g_ref, b_ref, wu_ref, wd_ref, o_ref, h_ref, *, alpha, rows):
    j = pl.program_id(1)
    last = pl.num_programs(1) - 1
    n_chunks = o_ref.shape[0] // rows

    def down(r):
        u = jnp.maximum(_dot(h_ref[r, :], wu_ref[...]), 0.0)
        return _dot((u * u).astype(BF16), wd_ref[...])

    @pl.when(j == 0)
    def _():
        sc1 = 1.0 + sc_ref[...]
        sh = sh_ref[...]
        for c in range(n_chunks):
            _modulate_rows(x_ref, sh, sc1, h_ref, c * rows, rows)
            r = slice(c * rows, (c + 1) * rows)
            o_ref[r, :] = down(r)

    @pl.when(jnp.logical_and(j > 0, j < last))
    def _():
        o_ref[...] += down(slice(None))

    @pl.when(j == last)
    def _():
        gate = gate_ref[...]
        g = g_ref[...]
        b = b_ref[...]
        for c in range(n_chunks):
            r = slice(c * rows, (c + 1) * rows)
            _residual_norm_rows(x_ref, o_ref[r, :] + down(r), gate, g, b, o_ref, c * rows, alpha)


def _mlp(x, mod3, layer, shift_blk, scale_blk, gate_blk, ln_g, ln_b, ln_row, w_up, w_down, alpha,
         rows_per_mod, fixed_row, tm=512, th=1024, rows=256):
    m, d = x.shape
    hid = w_up.shape[2]
    assert hid // th >= 2
    row = _mod_row_map(layer, rows_per_mod, tm, fixed_row)
    mod_spec = lambda blk: pl.BlockSpec((None, 1, d), lambda i, j: (row(i), 0, blk))
    vec = pl.BlockSpec((None, 1, d), lambda i, j: (ln_row, 0, 0))
    return pl.pallas_call(
        functools.partial(_mlp_kernel, alpha=alpha, rows=rows),
        grid=(m // tm, hid // th),
        in_specs=[
            pl.BlockSpec((tm, d), lambda i, j: (i, 0)),
            mod_spec(shift_blk), mod_spec(scale_blk), mod_spec(gate_blk),
            vec, vec,
            pl.BlockSpec((None, d, th), lambda i, j: (layer, 0, j)),
            pl.BlockSpec((None, th, d), lambda i, j: (layer, j, 0)),
        ],
        out_specs=pl.BlockSpec((tm, d), lambda i, j: (i, 0)),
        out_shape=jax.ShapeDtypeStruct((m, d), F32),
        scratch_shapes=[pltpu.VMEM((tm, d), BF16)],
        compiler_params=_params("parallel", "arbitrary"),
        name="mlp",
    )(x, mod3, mod3, mod3, ln_g, ln_b, w_up, w_down)


def _rope_tables(seq):
    t = jnp.arange(seq)
    row = (t // GRID_W).astype(F32)
    col = (t % GRID_W).astype(F32)
    axis_dim = HEAD_DIM // 2
    freqs = ROPE_THETA ** (-jnp.arange(0, axis_dim, 2, dtype=F32) / axis_dim)
    ar, ac = row[:, None] * freqs, col[:, None] * freqs
    cos = jnp.concatenate([jnp.cos(ar), jnp.cos(ar), jnp.cos(ac), jnp.cos(ac)], axis=-1)
    sin = jnp.concatenate([-jnp.sin(ar), jnp.sin(ar), -jnp.sin(ac), jnp.sin(ac)], axis=-1)
    return cos, sin


def kernel(x, c, ctx, c_ctx, w_mod, b_mod, w_in, rpb, q_gain, k_gain, conv_w, w_branch, w_o, w_up, w_down,
           ln_g, ln_b):
    batch, seq, d = x.shape
    ctx_len = ctx.shape[1]
    depth = w_mod.shape[0]
    assert d == D_MODEL and batch < MOD_ROWS and seq % NA_QBLK == 0 and seq // GRID_W >= NA_K_ROWS
    alpha = float((2 * depth) ** 0.25)
    ctx_row = batch

    c_all = jnp.concatenate([c, c_ctx[None, :], jnp.zeros((MOD_ROWS - batch - 1, d), F32)], axis=0)
    mod3 = _modulation(c_all, w_mod, b_mod).reshape(depth * MOD_ROWS, 1, 6 * d)
    cos, sin = _rope_tables(seq)
    plan = _na_plan(seq // GRID_W)
    bias = _na_bias(rpb.reshape((depth * NA_HEADS,) + rpb.shape[2:]), plan[2])

    w_in_b, wb_b, wo_b, wu_b, wd_b = (w.astype(BF16) for w in (w_in, w_branch, w_o, w_up, w_down))
    qg = q_gain.reshape(depth, 1, HEAD_DIM)
    kg = k_gain.reshape(depth, 1, HEAD_DIM)
    lg = ln_g.reshape(depth * 2, 1, d)
    lb = ln_b.reshape(depth * 2, 1, d)

    xl = x.reshape(batch * seq, d)
    xc = ctx.reshape(batch * ctx_len, d)
    for l in range(depth):
        with_ctx_out = l < depth - 1

        z = _ln_matmul(xl, mod3, l, 0, 1, w_in_b, Z_ROT, IN_W, seq, None, tm=1024, tn=1536)
        if with_ctx_out:
            zc = _ln_matmul(xc, mod3, l, 0, 1, w_in_b, Z_ROT, IN_W, None, ctx_row, tm=512, tn=1536)
            ctx_cols = Z_COLS
        else:
            zc = _ln_matmul(xc, mod3, l, 0, 1, w_in_b, KV_COLS.start, KV_COLS.width, None, ctx_row, tm=512, tn=512)
            ctx_cols = KV_COLS

        o_na = _na_attention(z, zc, ctx_cols, bias, l, plan, batch, seq, ctx_len)
        o_gqa = _gqa_attention(z, zc, ctx_cols, qg, kg, l, cos, sin, batch, seq, ctx_len)
        m_act = _merge(o_na, o_gqa, z, conv_w, wb_b, l, seq)
        xl = _out_proj(m_act, wo_b, xl, mod3, l, 2, lg, lb, 2 * l, alpha, seq, None)
        xl = _mlp(xl, mod3, l, 3, 4, 5, lg, lb, 2 * l + 1, wu_b, wd_b, alpha, seq, None)

        if with_ctx_out:
            o_na_c = _ctx_attention(zc, W_QA, W_KA, W_VA, 1, qg, kg, l, False, batch, ctx_len)
            o_gqa_c = _ctx_attention(zc, W_QB, W_KB, W_VB, GQA_GROUP, qg, kg, l, True, batch, ctx_len)
            m_c = _merge(o_na_c, o_gqa_c, zc, conv_w, wb_b, l, ctx_len)
            xc = _out_proj(m_c, wo_b, xc, mod3, l, 2, lg, lb, 2 * l, alpha, None, ctx_row)
            xc = _mlp(xc, mod3, l, 3, 4, 5, lg, lb, 2 * l + 1, wu_b, wd_b, alpha, None, ctx_row)
    return xl.reshape(batch, seq, d)
```

```python
import functools

import numpy as np
import jax
import jax.numpy as jnp
from jax import lax
from jax.experimental import pallas as pl
from jax.experimental.pallas import tpu as pltpu

D_MODEL = 2048
GRID_W = 64
HEAD_DIM = 128
BRANCH_W = D_MODEL // 2
NA_HEADS = BRANCH_W // HEAD_DIM
GQA_HEADS = BRANCH_W // HEAD_DIM
GQA_KV_HEADS = GQA_HEADS // 4
GQA_GROUP = GQA_HEADS // GQA_KV_HEADS
KV_W = GQA_KV_HEADS * HEAD_DIM
CONV_K = 3
MLP_HIDDEN = 4 * D_MODEL
N_BRANCH = 3
WIN_R = 8
WIN_C = 16
ROPE_THETA = 10000.0
NORM_EPS = 1e-6
NEG_INF = -1e30
ATTN_SCALE = HEAD_DIM ** -0.5
LOG2E = 1.4426950408889634
SCORE_SCALE = ATTN_SCALE * LOG2E

W_QA = 0
W_QB = W_QA + BRANCH_W
W_KA = W_QB + BRANCH_W
W_VA = W_KA + BRANCH_W
W_KB = W_VA + BRANCH_W
W_VB = W_KB + KV_W
W_CB = W_VB + KV_W
W_CC = W_CB + BRANCH_W
W_CX = W_CC + BRANCH_W
W_G = W_CX + BRANCH_W
IN_W = W_G + N_BRANCH * D_MODEL
CONV_W = 3 * BRANCH_W
GATE_W = N_BRANCH * D_MODEL
Z_ROT = 1536


class _Cols:
    def __init__(self, start, width, rot):
        self.start, self.width, self.rot = start, width, rot

    def block(self, w_col, blk):
        off = (w_col - self.start - self.rot) % self.width
        assert off % blk == 0 and self.width % blk == 0
        n, base = self.width // blk, off // blk
        return lambda k: (base + k) % n

    def index(self, w_col, blk):
        off = (w_col - self.start - self.rot) % self.width
        assert off % blk == 0 and off + blk <= self.width
        return off // blk


Z_COLS = _Cols(0, IN_W, Z_ROT)
KV_COLS = _Cols(W_KA, W_CB - W_KA, 0)

MOD_ROWS = 32
NA_Q_ROWS = 4
NA_K_ROWS = 12
NA_QBLK = NA_Q_ROWS * GRID_W
NA_KWIN = NA_K_ROWS * GRID_W
SUBLANES = 8

VMEM_LIMIT_BYTES = 56 * 1024 * 1024
LN_CHUNK = 128

F32 = jnp.float32
BF16 = jnp.bfloat16


def _params(*semantics):
    return pltpu.CompilerParams(dimension_semantics=semantics, vmem_limit_bytes=VMEM_LIMIT_BYTES)


def _resident(block_shape, index_map):
    return pl.BlockSpec(block_shape, index_map, pipeline_mode=pl.Buffered(1))


def _layer_norm(x):
    mu = jnp.mean(x, axis=-1, keepdims=True)
    xc = x - mu
    var = jnp.mean(xc * xc, axis=-1, keepdims=True)
    return xc * lax.rsqrt(var + NORM_EPS)


def _rms_norm(x, gain):
    return x * lax.rsqrt(jnp.mean(x * x, axis=-1, keepdims=True) + NORM_EPS) * gain


def _rope(x, cos, sin_signed):
    lane = lax.broadcasted_iota(jnp.int32, x.shape, 1)
    partner = jnp.where((lane & 32) == 0, pltpu.roll(x, 96, axis=1), pltpu.roll(x, 32, axis=1))
    return x * cos + partner * sin_signed


def _dot(a, b):
    return jnp.dot(a, b, preferred_element_type=F32)


def _dot_nt(a, b):
    return lax.dot_general(a, b, (((1,), (1,)), ((), ())), preferred_element_type=F32)


def _modulate_rows(x_ref, sh, sc1, h_ref, r0, rows):
    for s in range(rows // LN_CHUNK):
        rr = slice(r0 + s * LN_CHUNK, r0 + (s + 1) * LN_CHUNK)
        h_ref[rr, :] = (_layer_norm(x_ref[rr, :]) * sc1 + sh).astype(h_ref.dtype)


def _residual_norm_rows(x_ref, f, gate, g, b, o_ref, r0, alpha):
    for s in range(f.shape[0] // LN_CHUNK):
        rr = slice(r0 + s * LN_CHUNK, r0 + (s + 1) * LN_CHUNK)
        y = alpha * x_ref[rr, :] + gate * f[s * LN_CHUNK:(s + 1) * LN_CHUNK, :]
        o_ref[rr, :] = _layer_norm(y) * g + b


def _mod_row_map(layer, rows_per_mod, tm, fixed_row):
    base = layer * MOD_ROWS
    if fixed_row is not None:
        return lambda i: base + fixed_row
    return lambda i: base + (i * tm) // rows_per_mod


def _mod_kernel(c_ref, w_ref, b_ref, o_ref):
    c = c_ref[...]
    a = (c * jax.nn.sigmoid(c)).astype(BF16)
    o_ref[...] = _dot(a, w_ref[...].astype(BF16)) + b_ref[...]


def _modulation(c_all, w_mod, b_mod, tn=1024):
    depth, d, n = w_mod.shape
    return pl.pallas_call(
        _mod_kernel,
        grid=(depth, n // tn),
        in_specs=[
            pl.BlockSpec((MOD_ROWS, d), lambda l, j: (0, 0)),
            pl.BlockSpec((None, d, tn), lambda l, j: (l, 0, j)),
            pl.BlockSpec((None, 1, tn), lambda l, j: (l, 0, j)),
        ],
        out_specs=pl.BlockSpec((None, MOD_ROWS, tn), lambda l, j: (l, 0, j)),
        out_shape=jax.ShapeDtypeStruct((depth, MOD_ROWS, n), F32),
        compiler_params=_params("parallel", "parallel"),
        name="modulation",
    )(c_all, w_mod, b_mod.reshape(depth, 1, n))


def _ln_matmul_kernel(x_ref, sh_ref, sc_ref, w_ref, o_ref, h_ref, *, rows):
    j = pl.program_id(1)

    @pl.when(j == 0)
    def _():
        sc1 = 1.0 + sc_ref[...]
        sh = sh_ref[...]
        for c in range(x_ref.shape[0] // rows):
            _modulate_rows(x_ref, sh, sc1, h_ref, c * rows, rows)
            r = slice(c * rows, (c + 1) * rows)
            o_ref[r, :] = _dot(h_ref[r, :], w_ref[...]).astype(o_ref.dtype)

    @pl.when(j > 0)
    def _():
        o_ref[...] = _dot(h_ref[...], w_ref[...]).astype(o_ref.dtype)


def _ln_matmul(x, mod3, layer, shift_blk, scale_blk, w, first_col, n, rows_per_mod, fixed_row, tm, tn, rows=256):
    m, d = x.shape
    w_tiles = w.shape[2] // tn
    assert first_col % tn == 0 and n % tn == 0 and w.shape[2] % tn == 0
    row = _mod_row_map(layer, rows_per_mod, tm, fixed_row)
    return pl.pallas_call(
        functools.partial(_ln_matmul_kernel, rows=rows),
        grid=(m // tm, n // tn),
        in_specs=[
            pl.BlockSpec((tm, d), lambda i, j: (i, 0)),
            pl.BlockSpec((None, 1, d), lambda i, j: (row(i), 0, shift_blk)),
            pl.BlockSpec((None, 1, d), lambda i, j: (row(i), 0, scale_blk)),
            pl.BlockSpec((None, d, tn), lambda i, j: (layer, 0, (first_col // tn + j) % w_tiles)),
        ],
        out_specs=pl.BlockSpec((tm, tn), lambda i, j: (i, j)),
        out_shape=jax.ShapeDtypeStruct((m, n), BF16),
        scratch_shapes=[pltpu.VMEM((tm, d), BF16)],
        compiler_params=_params("parallel", "arbitrary"),
        name="ln_matmul",
    )(x, mod3, mod3, w)


def _na_plan(n_rows):
    wr = min(WIN_R, n_rows)
    n_qblk = n_rows // NA_Q_ROWS
    r = np.arange(n_rows)
    row_start = np.clip(r - wr // 2, 0, n_rows - wr)
    starts, pat_ids, patterns, keys = [], [], [], {}
    for qi in range(n_qblk):
        k0 = int(np.clip(qi * NA_Q_ROWS - NA_Q_ROWS, 0, n_rows - NA_K_ROWS))
        q_rows = qi * NA_Q_ROWS + np.arange(NA_Q_ROWS)
        k_rows = k0 + np.arange(NA_K_ROWS)
        valid = (k_rows[None, :] >= row_start[q_rows][:, None]) & (k_rows[None, :] < row_start[q_rows][:, None] + wr)
        assert valid.sum(axis=1).min() == wr, "key window must cover every query row's neighbourhood"
        dr_idx = np.clip(k_rows[None, :] - q_rows[:, None] + WIN_R - 1, 0, 2 * WIN_R - 2)
        key = (valid.tobytes(), dr_idx.tobytes())
        if key not in keys:
            keys[key] = len(patterns)
            patterns.append((valid, dr_idx))
        starts.append(k0)
        pat_ids.append(keys[key])
    return starts, pat_ids, patterns


def _na_bias(rpb, patterns):
    n, n_dr, n_dc = rpb.shape
    col = np.arange(GRID_W)
    col_start = np.clip(col - WIN_C // 2, 0, GRID_W - WIN_C)
    in_win = (col[None, :] >= col_start[:, None]) & (col[None, :] < col_start[:, None] + WIN_C)
    assert np.abs(col[None, :] - col[:, None])[in_win].max() <= WIN_C - 1
    line = 2 * GRID_W
    left = GRID_W - 1 - (WIN_C - 1)
    w = jnp.pad(rpb.astype(F32), ((0, 0), (0, 0), (left, line - left - n_dc)))
    w = jnp.broadcast_to(w[:, :, None, :], (n, n_dr, GRID_W, line)).reshape(n, n_dr, GRID_W * line)
    toe = w[:, :, :GRID_W * (line - 1)].reshape(n, n_dr, GRID_W, line - 1)[..., GRID_W - 1:]
    toe = jnp.where(in_win[None, None], toe * LOG2E, NEG_INF)
    toe = toe.transpose(0, 2, 1, 3).reshape(n, GRID_W, n_dr * GRID_W)
    out = []
    for valid, dr_idx in patterns:
        per_row = []
        for qr in range(NA_Q_ROWS):
            kr = np.flatnonzero(valid[qr])
            a, b = int(kr[0]), int(kr[-1]) + 1
            d0 = int(dr_idx[qr, a])
            assert b - a == len(kr) and (dr_idx[qr, a:b] == d0 + np.arange(b - a)).all()
            piece = toe[:, :, d0 * GRID_W:(d0 + b - a) * GRID_W]
            per_row.append(jnp.pad(piece, ((0, 0), (0, 0), (a * GRID_W, (NA_K_ROWS - b) * GRID_W)),
                                   constant_values=NEG_INF))
        out.append(jnp.stack(per_row, axis=1).reshape(n, NA_QBLK, NA_KWIN))
    return jnp.stack(out, axis=1)


def _fill_values_with_ones(vs_ref, v, vc, seq):
    vs_ref[0:seq, 0:HEAD_DIM] = v
    vs_ref[seq:, 0:HEAD_DIM] = vc
    vs_ref[:, HEAD_DIM:] = jnp.ones((vs_ref.shape[0], HEAD_DIM), vs_ref.dtype)


def _normalised(o_aug):
    return o_aug[:, :HEAD_DIM] / o_aug[:, HEAD_DIM:]


def _na_kernel(q_ref, k_ref, v_ref, kc_ref, vc_ref, bias_ref, o_ref, vs_ref, *, starts, pat_ids, seq):
    for h in range(q_ref.shape[1] // HEAD_DIM):
        hc = slice(h * HEAD_DIM, (h + 1) * HEAD_DIM)
        vs = vs_ref.at[h]
        _fill_values_with_ones(vs, v_ref[:, hc], vc_ref[:, hc], seq)
        kc = kc_ref[:, hc]
        for qi, (k0, pat) in enumerate(zip(starts, pat_ids)):
            rows = slice(qi * NA_QBLK, (qi + 1) * NA_QBLK)
            keys = slice(k0 * GRID_W, k0 * GRID_W + NA_KWIN)
            q = q_ref[rows, hc]
            s_loc = _dot_nt(q, k_ref[keys, hc]) * SCORE_SCALE + bias_ref[h, pat]
            s_ctx = _dot_nt(q, kc) * SCORE_SCALE
            m = jnp.maximum(jnp.max(s_loc, axis=-1, keepdims=True), jnp.max(s_ctx, axis=-1, keepdims=True))
            p_loc = jnp.exp2(s_loc - m).astype(BF16)
            p_ctx = jnp.exp2(s_ctx - m).astype(BF16)
            o_aug = _dot(p_loc, vs[keys, :]) + _dot(p_ctx, vs[seq:, :])
            o_ref[rows, hc] = _normalised(o_aug).astype(o_ref.dtype)


def _na_attention(z, zc, ctx_cols, bias, layer, plan, batch, seq, ctx_len, heads=2):
    starts, pat_ids, patterns = plan
    hw = heads * HEAD_DIM
    steps = NA_HEADS // heads
    qa, ka, va = (Z_COLS.block(c, hw) for c in (W_QA, W_KA, W_VA))
    kc, vc = (ctx_cols.block(c, hw) for c in (W_KA, W_VA))
    kernel = functools.partial(_na_kernel, starts=tuple(starts), pat_ids=tuple(pat_ids), seq=seq)
    return pl.pallas_call(
        kernel,
        grid=(steps, batch),
        in_specs=[
            pl.BlockSpec((seq, hw), lambda h, b: (b, qa(h))),
            pl.BlockSpec((seq, hw), lambda h, b: (b, ka(h))),
            pl.BlockSpec((seq, hw), lambda h, b: (b, va(h))),
            pl.BlockSpec((ctx_len, hw), lambda h, b: (b, kc(h))),
            pl.BlockSpec((ctx_len, hw), lambda h, b: (b, vc(h))),
            pl.BlockSpec((heads, len(patterns), NA_QBLK, NA_KWIN), lambda h, b: (layer * steps + h, 0, 0, 0)),
        ],
        out_specs=pl.BlockSpec((seq, hw), lambda h, b: (b, h)),
        out_shape=jax.ShapeDtypeStruct((batch * seq, BRANCH_W), BF16),
        scratch_shapes=[pltpu.VMEM((heads, seq + ctx_len, 2 * HEAD_DIM), BF16)],
        compiler_params=_params("parallel", "parallel"),
        name="na_attention",
    )(z, z, z, zc, zc, bias)


def _gqa_kernel(q_ref, k_ref, v_ref, kc_ref, vc_ref, qg_ref, kg_ref, cos_ref, sin_ref, o_ref,
                ks_ref, vs_ref, *, tq, seq, qrows):
    qi = pl.program_id(2)

    @pl.when(qi == 0)
    def _():
        kg = kg_ref[...]
        kn = _rms_norm(k_ref[...].astype(F32), kg)
        ks_ref[0:seq, :] = _rope(kn, cos_ref[...], sin_ref[...]).astype(BF16)
        ks_ref[seq:, :] = _rms_norm(kc_ref[...].astype(F32), kg).astype(BF16)
        _fill_values_with_ones(vs_ref, v_ref[...], vc_ref[...], seq)

    qg = qg_ref[...]
    for c in range(tq // qrows):
        r = pl.multiple_of(qi * tq + c * qrows, qrows)
        cos = cos_ref[pl.ds(r, qrows), :]
        sin = sin_ref[pl.ds(r, qrows), :]
        rows = slice(c * qrows, (c + 1) * qrows)
        for g in range(GQA_GROUP):
            cols = slice(g * HEAD_DIM, (g + 1) * HEAD_DIM)
            q = q_ref[rows, cols].astype(F32)
            qn = (_rope(_rms_norm(q, qg), cos, sin) * SCORE_SCALE).astype(BF16)
            s = _dot_nt(qn, ks_ref[...])
            m = jnp.max(s, axis=-1, keepdims=True)
            p = jnp.exp2(s - m).astype(BF16)
            o_ref[rows, cols] = _normalised(_dot(p, vs_ref[...])).astype(o_ref.dtype)


def _gqa_attention(z, zc, ctx_cols, q_gain, k_gain, layer, cos, sin, batch, seq, ctx_len, tq=None, qrows=256):
    hb = HEAD_DIM
    gw = GQA_GROUP * HEAD_DIM
    tq = seq if tq is None else tq
    nq = seq // tq
    qb = Z_COLS.block(W_QB, gw)
    kb, vb = (Z_COLS.block(c, hb) for c in (W_KB, W_VB))
    kc, vc = (ctx_cols.block(c, hb) for c in (W_KB, W_VB))
    kernel = functools.partial(_gqa_kernel, tq=tq, seq=seq, qrows=qrows)
    gain = pl.BlockSpec((None, 1, hb), lambda b, kh, qi: (layer, 0, 0))
    return pl.pallas_call(
        kernel,
        grid=(batch, GQA_KV_HEADS, nq),
        in_specs=[
            pl.BlockSpec((tq, gw), lambda b, kh, qi: (b * nq + qi, qb(kh))),
            pl.BlockSpec((seq, hb), lambda b, kh, qi: (b, kb(kh))),
            pl.BlockSpec((seq, hb), lambda b, kh, qi: (b, vb(kh))),
            pl.BlockSpec((ctx_len, hb), lambda b, kh, qi: (b, kc(kh))),
            pl.BlockSpec((ctx_len, hb), lambda b, kh, qi: (b, vc(kh))),
            gain, gain,
            pl.BlockSpec((seq, hb), lambda b, kh, qi: (0, 0)),
            pl.BlockSpec((seq, hb), lambda b, kh, qi: (0, 0)),
        ],
        out_specs=pl.BlockSpec((tq, gw), lambda b, kh, qi: (b * nq + qi, kh)),
        out_shape=jax.ShapeDtypeStruct((batch * seq, BRANCH_W), BF16),
        scratch_shapes=[pltpu.VMEM((seq + ctx_len, hb), BF16), pltpu.VMEM((seq + ctx_len, 2 * hb), BF16)],
        compiler_params=_params("parallel", "parallel", "arbitrary"),
        name="gqa_attention",
    )(z, z, z, zc, zc, q_gain, k_gain, cos, sin)


def _ctx_attn_kernel(q_ref, k_ref, v_ref, qg_ref, kg_ref, o_ref, *, qk_norm):
    shared = k_ref.shape[1] == HEAD_DIM
    for h in range(q_ref.shape[1] // HEAD_DIM):
        hc = slice(h * HEAD_DIM, (h + 1) * HEAD_DIM)
        kvc = slice(0, HEAD_DIM) if shared else hc
        q = q_ref[:, hc]
        k = k_ref[:, kvc]
        if qk_norm:
            q = _rms_norm(q.astype(F32), qg_ref[...]).astype(BF16)
            k = _rms_norm(k.astype(F32), kg_ref[...]).astype(BF16)
        s = _dot_nt(q, k) * ATTN_SCALE
        m = jnp.max(s, axis=-1, keepdims=True)
        p = jnp.exp(s - m)
        den = jnp.sum(p, axis=-1, keepdims=True)
        o_ref[:, hc] = (_dot(p.astype(BF16), v_ref[:, kvc]) / den).astype(o_ref.dtype)


def _ctx_attention(zc, w_q, w_k, w_v, shared_kv, q_gain, k_gain, layer, qk_norm, batch, ctx_len):
    gw = GQA_GROUP * HEAD_DIM
    kvw = HEAD_DIM if shared_kv else gw
    qc = Z_COLS.block(w_q, gw)
    kc, vc = (Z_COLS.block(c, kvw) for c in (w_k, w_v))
    kernel = functools.partial(_ctx_attn_kernel, qk_norm=qk_norm)
    gain = pl.BlockSpec((None, 1, HEAD_DIM), lambda b, g: (layer, 0, 0))
    return pl.pallas_call(
        kernel,
        grid=(batch, BRANCH_W // gw),
        in_specs=[
            pl.BlockSpec((ctx_len, gw), lambda b, g: (b, qc(g))),
            pl.BlockSpec((ctx_len, kvw), lambda b, g: (b, kc(g))),
            pl.BlockSpec((ctx_len, kvw), lambda b, g: (b, vc(g))),
            gain, gain,
        ],
        out_specs=pl.BlockSpec((ctx_len, gw), lambda b, g: (b, g)),
        out_shape=jax.ShapeDtypeStruct((batch * ctx_len, BRANCH_W), BF16),
        compiler_params=_params("parallel", "parallel"),
        name="ctx_attention",
    )(zc, zc, zc, q_gain, k_gain)


def _conv_tile(c_ref, prev_ref, next_ref, w_ref, *, seq_len, tc):
    tm = c_ref.shape[0]
    chunks = []
    r0 = pl.program_id(0) * tm
    t = (r0 + lax.broadcasted_iota(jnp.int32, (tm, 1), 0)) & (seq_len - 1)
    row = lax.broadcasted_iota(jnp.int32, (tm, 1), 0)
    at_start, at_end = t == 0, t == seq_len - 1
    w = w_ref[...]
    for j in range(BRANCH_W // tc):
        cb = slice(j * tc, (j + 1) * tc)
        cc = slice(BRANCH_W + j * tc, BRANCH_W + (j + 1) * tc)
        cx = slice(2 * BRANCH_W + j * tc, 2 * BRANCH_W + (j + 1) * tc)
        v = c_ref[:, cc].astype(F32) * c_ref[:, cx].astype(F32)
        above = prev_ref[SUBLANES - 1:SUBLANES, cc].astype(F32) * prev_ref[SUBLANES - 1:SUBLANES, cx].astype(F32)
        below = next_ref[0:1, cc].astype(F32) * next_ref[0:1, cx].astype(F32)
        prev = jnp.where(row == 0, above, pltpu.roll(v, 1, axis=0))
        nxt = jnp.where(row == tm - 1, below, pltpu.roll(v, tm - 1, axis=0))
        prev = jnp.where(at_start, 0.0, prev)
        nxt = jnp.where(at_end, 0.0, nxt)
        y = w[0:1, cb] * prev + w[1:2, cb] * v + w[2:3, cb] * nxt
        chunks.append((c_ref[:, cb].astype(F32) * y).astype(BF16))
    return jnp.concatenate(chunks, axis=1)


def _merge_kernel(a0_ref, a1_ref, c_ref, prev_ref, next_ref, cw_ref, g_ref, w_ref, o_ref, *, tn, seq_len, tc):
    d = o_ref.shape[1]
    branches = (a0_ref[...], a1_ref[...], _conv_tile(c_ref, prev_ref, next_ref, cw_ref, seq_len=seq_len, tc=tc))
    for j in range(d // tn):
        cols = slice(j * tn, (j + 1) * tn)
        acc = None
        for k, a in enumerate(branches):
            gate = jax.nn.sigmoid(g_ref[:, k * d + j * tn:k * d + (j + 1) * tn].astype(F32))
            term = gate * _dot(a, w_ref[k, :, cols])
            acc = term if acc is None else acc + term
        o_ref[:, cols] = acc.astype(o_ref.dtype)


def _merge(o_na, o_gqa, z, conv_w, w_branch, layer, seq_len, tm=512, tn=512, tc=256):
    m = o_na.shape[0]
    assert seq_len & (seq_len - 1) == 0 and (tm % seq_len == 0 or seq_len % tm == 0) and m % seq_len == 0
    conv_blk, gate_blk = Z_COLS.index(W_CB, CONV_W), Z_COLS.index(W_G, GATE_W)
    halo = tm // SUBLANES
    last_halo = m // SUBLANES - 1
    act = pl.BlockSpec((tm, BRANCH_W), lambda i: (i, 0))
    return pl.pallas_call(
        functools.partial(_merge_kernel, tn=tn, seq_len=seq_len, tc=tc),
        grid=(m // tm,),
        in_specs=[act, act,
                  pl.BlockSpec((tm, CONV_W), lambda i: (i, conv_blk)),
                  pl.BlockSpec((SUBLANES, CONV_W), lambda i: (jnp.maximum(i * halo - 1, 0), conv_blk)),
                  pl.BlockSpec((SUBLANES, CONV_W), lambda i: (jnp.minimum((i + 1) * halo, last_halo), conv_blk)),
                  pl.BlockSpec((None, CONV_K, BRANCH_W), lambda i: (layer, 0, 0)),
                  pl.BlockSpec((tm, GATE_W), lambda i: (i, gate_blk)),
                  _resident((None, N_BRANCH, BRANCH_W, D_MODEL), lambda i: (layer, 0, 0, 0))],
        out_specs=pl.BlockSpec((tm, D_MODEL), lambda i: (i, 0)),
        out_shape=jax.ShapeDtypeStruct((m, D_MODEL), BF16),
        compiler_params=_params("parallel"),
        name="merge",
    )(o_na, o_gqa, z, z, z, conv_w, z, w_branch)


def _out_proj_kernel(m_ref, w_ref, x_ref, gate_ref, g_ref, b_ref, o_ref, *, alpha, rows):
    gate = gate_ref[...]
    g = g_ref[...]
    b = b_ref[...]
    for c in range(o_ref.shape[0] // rows):
        f = _dot(m_ref[c * rows:(c + 1) * rows, :], w_ref[...])
        _residual_norm_rows(x_ref, f, gate, g, b, o_ref, c * rows, alpha)


def _out_proj(m_act, w_o, x, mod3, layer, gate_blk, ln_g, ln_b, ln_row, alpha, rows_per_mod, fixed_row,
              tm=512, rows=256):
    m, d = x.shape
    row = _mod_row_map(layer, rows_per_mod, tm, fixed_row)
    vec = pl.BlockSpec((None, 1, d), lambda i: (ln_row, 0, 0))
    return pl.pallas_call(
        functools.partial(_out_proj_kernel, alpha=alpha, rows=rows),
        grid=(m // tm,),
        in_specs=[
            pl.BlockSpec((tm, d), lambda i: (i, 0)),
            _resident((None, d, d), lambda i: (layer, 0, 0)),
            pl.BlockSpec((tm, d), lambda i: (i, 0)),
            pl.BlockSpec((None, 1, d), lambda i: (row(i), 0, gate_blk)),
            vec, vec,
        ],
        out_specs=pl.BlockSpec((tm, d), lambda i: (i, 0)),
        out_shape=jax.ShapeDtypeStruct((m, d), F32),
        compiler_params=_params("parallel"),
        name="out_proj",
    )(m_act, w_o, x, mod3, ln_g, ln_b)


def _mlp_kernel(x_ref, sh_ref, sc_ref, gate_ref, g_ref, b_ref, wu_ref, wd_ref, o_ref, h_ref, *, alpha, rows):
    j = pl.program_id(1)
    last = pl.num_programs(1) - 1
    n_chunks = o_ref.shape[0] // rows

    def down(r):
        u = jnp.maximum(_dot(h_ref[r, :], wu_ref[...]), 0.0)
        return _dot((u * u).astype(BF16), wd_ref[...])

    @pl.when(j == 0)
    def _():
        sc1 = 1.0 + sc_ref[...]
        sh = sh_ref[...]
        for c in range(n_chunks):
            _modulate_rows(x_ref, sh, sc1, h_ref, c * rows, rows)
            r = slice(c * rows, (c + 1) * rows)
            o_ref[r, :] = down(r)

    @pl.when(jnp.logical_and(j > 0, j < last))
    def _():
        o_ref[...] += down(slice(None))

    @pl.when(j == last)
    def _():
        gate = gate_ref[...]
        g = g_ref[...]
        b = b_ref[...]
        for c in range(n_chunks):
            r = slice(c * rows, (c + 1) * rows)
            _residual_norm_rows(x_ref, o_ref[r, :] + down(r), gate, g, b, o_ref, c * rows, alpha)


def _mlp(x, mod3, layer, shift_blk, scale_blk, gate_blk, ln_g, ln_b, ln_row, w_up, w_down, alpha,
         rows_per_mod, fixed_row, tm=512, th=1024, rows=256):
    m, d = x.shape
    hid = w_up.shape[2]
    assert hid // th >= 2
    row = _mod_row_map(layer, rows_per_mod, tm, fixed_row)
    mod_spec = lambda blk: pl.BlockSpec((None, 1, d), lambda i, j: (row(i), 0, blk))
    vec = pl.BlockSpec((None, 1, d), lambda i, j: (ln_row, 0, 0))
    return pl.pallas_call(
        functools.partial(_mlp_kernel, alpha=alpha, rows=rows),
        grid=(m // tm, hid // th),
        in_specs=[
            pl.BlockSpec((tm, d), lambda i, j: (i, 0)),
            mod_spec(shift_blk), mod_spec(scale_blk), mod_spec(gate_blk),
            vec, vec,
            pl.BlockSpec((None, d, th), lambda i, j: (layer, 0, j)),
            pl.BlockSpec((None, th, d), lambda i, j: (layer, j, 0)),
        ],
        out_specs=pl.BlockSpec((tm, d), lambda i, j: (i, 0)),
        out_shape=jax.ShapeDtypeStruct((m, d), F32),
        scratch_shapes=[pltpu.VMEM((tm, d), BF16)],
        compiler_params=_params("parallel", "arbitrary"),
        name="mlp",
    )(x, mod3, mod3, mod3, ln_g, ln_b, w_up, w_down)


def _rope_tables(seq):
    t = jnp.arange(seq)
    row = (t // GRID_W).astype(F32)
    col = (t % GRID_W).astype(F32)
    axis_dim = HEAD_DIM // 2
    freqs = ROPE_THETA ** (-jnp.arange(0, axis_dim, 2, dtype=F32) / axis_dim)
    ar, ac = row[:, None] * freqs, col[:, None] * freqs
    cos = jnp.concatenate([jnp.cos(ar), jnp.cos(ar), jnp.cos(ac), jnp.cos(ac)], axis=-1)
    sin = jnp.concatenate([-jnp.sin(ar), jnp.sin(ar), -jnp.sin(ac), jnp.sin(ac)], axis=-1)
    return cos, sin


def kernel(x, c, ctx, c_ctx, w_mod, b_mod, w_in, rpb, q_gain, k_gain, conv_w, w_branch, w_o, w_up, w_down,
           ln_g, ln_b):
    batch, seq, d = x.shape
    ctx_len = ctx.shape[1]
    depth = w_mod.shape[0]
    assert d == D_MODEL and batch < MOD_ROWS and seq % NA_QBLK == 0 and seq // GRID_W >= NA_K_ROWS
    alpha = float((2 * depth) ** 0.25)
    ctx_row = batch

    c_all = jnp.concatenate([c, c_ctx[None, :], jnp.zeros((MOD_ROWS - batch - 1, d), F32)], axis=0)
    mod3 = _modulation(c_all, w_mod, b_mod).reshape(depth * MOD_ROWS, 1, 6 * d)
    cos, sin = _rope_tables(seq)
    plan = _na_plan(seq // GRID_W)
    bias = _na_bias(rpb.reshape((depth * NA_HEADS,) + rpb.shape[2:]), plan[2])

    w_in_b, wb_b, wo_b, wu_b, wd_b = (w.astype(BF16) for w in (w_in, w_branch, w_o, w_up, w_down))
    qg = q_gain.reshape(depth, 1, HEAD_DIM)
    kg = k_gain.reshape(depth, 1, HEAD_DIM)
    lg = ln_g.reshape(depth * 2, 1, d)
    lb = ln_b.reshape(depth * 2, 1, d)

    xl = x.reshape(batch * seq, d)
    xc = ctx.reshape(batch * ctx_len, d)
    for l in range(depth):
        with_ctx_out = l < depth - 1

        z = _ln_matmul(xl, mod3, l, 0, 1, w_in_b, Z_ROT, IN_W, seq, None, tm=1024, tn=1536)
        if with_ctx_out:
            zc = _ln_matmul(xc, mod3, l, 0, 1, w_in_b, Z_ROT, IN_W, None, ctx_row, tm=512, tn=1536)
            ctx_cols = Z_COLS
        else:
            zc = _ln_matmul(xc, mod3, l, 0, 1, w_in_b, KV_COLS.start, KV_COLS.width, None, ctx_row, tm=512, tn=512)
            ctx_cols = KV_COLS

        o_na = _na_attention(z, zc, ctx_cols, bias, l, plan, batch, seq, ctx_len)
        o_gqa = _gqa_attention(z, zc, ctx_cols, qg, kg, l, cos, sin, batch, seq, ctx_len)
        m_act = _merge(o_na, o_gqa, z, conv_w, wb_b, l, seq)
        xl = _out_proj(m_act, wo_b, xl, mod3, l, 2, lg, lb, 2 * l, alpha, seq, None)
        xl = _mlp(xl, mod3, l, 3, 4, 5, lg, lb, 2 * l + 1, wu_b, wd_b, alpha, seq, None)

        if with_ctx_out:
            o_na_c = _ctx_attention(zc, W_QA, W_KA, W_VA, False, qg, kg, l, False, batch, ctx_len)
            o_gqa_c = _ctx_attention(zc, W_QB, W_KB, W_VB, True, qg, kg, l, True, batch, ctx_len)
            m_c = _merge(o_na_c, o_gqa_c, zc, conv_w, wb_b, l, ctx_len)
            xc = _out_proj(m_c, wo_b, xc, mod3, l, 2, lg, lb, 2 * l, alpha, None, ctx_row)
            xc = _mlp(xc, mod3, l, 3, 4, 5, lg, lb, 2 * l + 1, wu_b, wd_b, alpha, None, ctx_row)
    return xl.reshape(batch, seq, d)
```

```python
import functools

import numpy as np
import jax
import jax.numpy as jnp
from jax import lax
from jax.experimental import pallas as pl
from jax.experimental.pallas import tpu as pltpu

D_MODEL = 2048
GRID_W = 64
HEAD_DIM = 128
BRANCH_W = D_MODEL // 2
NA_HEADS = BRANCH_W // HEAD_DIM
GQA_HEADS = BRANCH_W // HEAD_DIM
GQA_KV_HEADS = GQA_HEADS // 4
GQA_GROUP = GQA_HEADS // GQA_KV_HEADS
KV_W = GQA_KV_HEADS * HEAD_DIM
CONV_K = 3
MLP_HIDDEN = 4 * D_MODEL
N_BRANCH = 3
WIN_R = 8
WIN_C = 16
ROPE_THETA = 10000.0
NORM_EPS = 1e-6
NEG_INF = -1e30
ATTN_SCALE = HEAD_DIM ** -0.5
LOG2E = 1.4426950408889634
SCORE_SCALE = ATTN_SCALE * LOG2E

W_QA = 0
W_QB = W_QA + BRANCH_W
W_KA = W_QB + BRANCH_W
W_VA = W_KA + BRANCH_W
W_KB = W_VA + BRANCH_W
W_VB = W_KB + KV_W
W_CB = W_VB + KV_W
W_CC = W_CB + BRANCH_W
W_CX = W_CC + BRANCH_W
W_G = W_CX + BRANCH_W
IN_W = W_G + N_BRANCH * D_MODEL
CONV_W = 3 * BRANCH_W
GATE_W = N_BRANCH * D_MODEL
Z_ROT = 1536


class _Cols:
    def __init__(self, start, width, rot):
        self.start, self.width, self.rot = start, width, rot

    def block(self, w_col, blk):
        off = (w_col - self.start - self.rot) % self.width
        assert off % blk == 0 and self.width % blk == 0
        n, base = self.width // blk, off // blk
        return lambda k: (base + k) % n

    def index(self, w_col, blk):
        off = (w_col - self.start - self.rot) % self.width
        assert off % blk == 0 and off + blk <= self.width
        return off // blk


Z_COLS = _Cols(0, IN_W, Z_ROT)
KV_COLS = _Cols(W_KA, W_CB - W_KA, 0)

MOD_ROWS = 32
NA_Q_ROWS = 4
NA_K_ROWS = 12
NA_QBLK = NA_Q_ROWS * GRID_W
NA_KWIN = NA_K_ROWS * GRID_W
SUBLANES = 8

VMEM_LIMIT_BYTES = 56 * 1024 * 1024
LN_CHUNK = 128

F32 = jnp.float32
BF16 = jnp.bfloat16


def _params(*semantics):
    return pltpu.CompilerParams(dimension_semantics=semantics, vmem_limit_bytes=VMEM_LIMIT_BYTES)


def _resident(block_shape, index_map):
    return pl.BlockSpec(block_shape, index_map, pipeline_mode=pl.Buffered(1))


def _layer_norm(x, eps=NORM_EPS):
    mu = jnp.mean(x, axis=-1, keepdims=True)
    xc = x - mu
    var = jnp.mean(xc * xc, axis=-1, keepdims=True)
    return xc * lax.rsqrt(var + eps)


def _rms_norm(x, gain):
    return x * lax.rsqrt(jnp.mean(x * x, axis=-1, keepdims=True) + NORM_EPS) * gain


def _rope(x, cos, sin_signed):
    lane = lax.broadcasted_iota(jnp.int32, x.shape, 1)
    partner = jnp.where((lane & 32) == 0, pltpu.roll(x, 96, axis=1), pltpu.roll(x, 32, axis=1))
    return x * cos + partner * sin_signed


def _dot(a, b):
    return jnp.dot(a, b, preferred_element_type=F32)


def _dot_nt(a, b):
    return lax.dot_general(a, b, (((1,), (1,)), ((), ())), preferred_element_type=F32)


def _modulate_rows(x_ref, sh, sc1, h_ref, r0, rows):
    for s in range(rows // LN_CHUNK):
        rr = slice(r0 + s * LN_CHUNK, r0 + (s + 1) * LN_CHUNK)
        h_ref[rr, :] = (_layer_norm(x_ref[rr, :]) * sc1 + sh).astype(h_ref.dtype)


def _residual_norm_rows(x_ref, f, gate, g, b, o_ref, r0, alpha):
    gate_a = gate * (1.0 / alpha)
    eps_a = NORM_EPS / (alpha * alpha)
    for s in range(f.shape[0] // LN_CHUNK):
        rr = slice(r0 + s * LN_CHUNK, r0 + (s + 1) * LN_CHUNK)
        y = x_ref[rr, :] + gate_a * f[s * LN_CHUNK:(s + 1) * LN_CHUNK, :]
        o_ref[rr, :] = _layer_norm(y, eps_a) * g + b


def _mod_row_map(layer, rows_per_mod, tm, fixed_row):
    base = layer * MOD_ROWS
    if fixed_row is not None:
        return lambda i: base + fixed_row
    return lambda i: base + (i * tm) // rows_per_mod


def _mod_kernel(c_ref, w_ref, b_ref, o_ref):
    c = c_ref[...]
    a = (c * jax.nn.sigmoid(c)).astype(BF16)
    o_ref[...] = _dot(a, w_ref[...].astype(BF16)) + b_ref[...]


def _modulation(c_all, w_mod, b_mod, tn=1024):
    depth, d, n = w_mod.shape
    return pl.pallas_call(
        _mod_kernel,
        grid=(depth, n // tn),
        in_specs=[
            pl.BlockSpec((MOD_ROWS, d), lambda l, j: (0, 0)),
            pl.BlockSpec((None, d, tn), lambda l, j: (l, 0, j)),
            pl.BlockSpec((None, 1, tn), lambda l, j: (l, 0, j)),
        ],
        out_specs=pl.BlockSpec((None, MOD_ROWS, tn), lambda l, j: (l, 0, j)),
        out_shape=jax.ShapeDtypeStruct((depth, MOD_ROWS, n), F32),
        compiler_params=_params("parallel", "parallel"),
        name="modulation",
    )(c_all, w_mod, b_mod.reshape(depth, 1, n))


def _ln_matmul_kernel(x_ref, sh_ref, sc_ref, w_ref, o_ref, h_ref, *, rows):
    j = pl.program_id(1)

    @pl.when(j == 0)
    def _():
        sc1 = 1.0 + sc_ref[...]
        sh = sh_ref[...]
        for c in range(x_ref.shape[0] // rows):
            _modulate_rows(x_ref, sh, sc1, h_ref, c * rows, rows)
            r = slice(c * rows, (c + 1) * rows)
            o_ref[r, :] = _dot(h_ref[r, :], w_ref[...]).astype(o_ref.dtype)

    @pl.when(j > 0)
    def _():
        o_ref[...] = _dot(h_ref[...], w_ref[...]).astype(o_ref.dtype)


def _ln_matmul(x, mod3, layer, shift_blk, scale_blk, w, first_col, n, rows_per_mod, fixed_row, tm, tn, rows=256):
    m, d = x.shape
    w_tiles = w.shape[2] // tn
    assert first_col % tn == 0 and n % tn == 0 and w.shape[2] % tn == 0
    row = _mod_row_map(layer, rows_per_mod, tm, fixed_row)
    return pl.pallas_call(
        functools.partial(_ln_matmul_kernel, rows=rows),
        grid=(m // tm, n // tn),
        in_specs=[
            pl.BlockSpec((tm, d), lambda i, j: (i, 0)),
            pl.BlockSpec((None, 1, d), lambda i, j: (row(i), 0, shift_blk)),
            pl.BlockSpec((None, 1, d), lambda i, j: (row(i), 0, scale_blk)),
            pl.BlockSpec((None, d, tn), lambda i, j: (layer, 0, (first_col // tn + j) % w_tiles)),
        ],
        out_specs=pl.BlockSpec((tm, tn), lambda i, j: (i, j)),
        out_shape=jax.ShapeDtypeStruct((m, n), BF16),
        scratch_shapes=[pltpu.VMEM((tm, d), BF16)],
        compiler_params=_params("parallel", "arbitrary"),
        name="ln_matmul",
    )(x, mod3, mod3, w)


def _na_plan(n_rows):
    wr = min(WIN_R, n_rows)
    n_qblk = n_rows // NA_Q_ROWS
    r = np.arange(n_rows)
    row_start = np.clip(r - wr // 2, 0, n_rows - wr)
    starts, pat_ids, patterns, keys = [], [], [], {}
    for qi in range(n_qblk):
        k0 = int(np.clip(qi * NA_Q_ROWS - NA_Q_ROWS, 0, n_rows - NA_K_ROWS))
        q_rows = qi * NA_Q_ROWS + np.arange(NA_Q_ROWS)
        k_rows = k0 + np.arange(NA_K_ROWS)
        valid = (k_rows[None, :] >= row_start[q_rows][:, None]) & (k_rows[None, :] < row_start[q_rows][:, None] + wr)
        assert valid.sum(axis=1).min() == wr, "key window must cover every query row's neighbourhood"
        dr_idx = np.clip(k_rows[None, :] - q_rows[:, None] + WIN_R - 1, 0, 2 * WIN_R - 2)
        key = (valid.tobytes(), dr_idx.tobytes())
        if key not in keys:
            keys[key] = len(patterns)
            patterns.append((valid, dr_idx))
        starts.append(k0)
        pat_ids.append(keys[key])
    return starts, pat_ids, patterns


def _na_bias(rpb, patterns):
    n, n_dr, n_dc = rpb.shape
    col = np.arange(GRID_W)
    col_start = np.clip(col - WIN_C // 2, 0, GRID_W - WIN_C)
    in_win = (col[None, :] >= col_start[:, None]) & (col[None, :] < col_start[:, None] + WIN_C)
    assert np.abs(col[None, :] - col[:, None])[in_win].max() <= WIN_C - 1
    line = 2 * GRID_W
    left = GRID_W - 1 - (WIN_C - 1)
    w = jnp.pad(rpb.astype(F32), ((0, 0), (0, 0), (left, line - left - n_dc)))
    w = jnp.broadcast_to(w[:, :, None, :], (n, n_dr, GRID_W, line)).reshape(n, n_dr, GRID_W * line)
    toe = w[:, :, :GRID_W * (line - 1)].reshape(n, n_dr, GRID_W, line - 1)[..., GRID_W - 1:]
    toe = jnp.where(in_win[None, None], toe * LOG2E, NEG_INF)
    toe = toe.transpose(0, 2, 1, 3).reshape(n, GRID_W, n_dr * GRID_W)
    out = []
    for valid, dr_idx in patterns:
        per_row = []
        for qr in range(NA_Q_ROWS):
            kr = np.flatnonzero(valid[qr])
            a, b = int(kr[0]), int(kr[-1]) + 1
            d0 = int(dr_idx[qr, a])
            assert b - a == len(kr) and (dr_idx[qr, a:b] == d0 + np.arange(b - a)).all()
            piece = toe[:, :, d0 * GRID_W:(d0 + b - a) * GRID_W]
            per_row.append(jnp.pad(piece, ((0, 0), (0, 0), (a * GRID_W, (NA_K_ROWS - b) * GRID_W)),
                                   constant_values=NEG_INF))
        out.append(jnp.stack(per_row, axis=1).reshape(n, NA_QBLK, NA_KWIN))
    return jnp.stack(out, axis=1)


def _fill_values_with_ones(vs_ref, v, vc, seq):
    vs_ref[0:seq, 0:HEAD_DIM] = v
    vs_ref[seq:, 0:HEAD_DIM] = vc
    vs_ref[:, HEAD_DIM:] = jnp.ones((vs_ref.shape[0], HEAD_DIM), vs_ref.dtype)


def _normalised(o_aug):
    return o_aug[:, :HEAD_DIM] / o_aug[:, HEAD_DIM:]


def _na_kernel(q_ref, k_ref, v_ref, kc_ref, vc_ref, bias_ref, o_ref, vs_ref, *, starts, pat_ids, seq):
    for h in range(q_ref.shape[1] // HEAD_DIM):
        hc = slice(h * HEAD_DIM, (h + 1) * HEAD_DIM)
        vs = vs_ref.at[h]
        _fill_values_with_ones(vs, v_ref[:, hc], vc_ref[:, hc], seq)
        kc = kc_ref[:, hc]
        for qi, (k0, pat) in enumerate(zip(starts, pat_ids)):
            rows = slice(qi * NA_QBLK, (qi + 1) * NA_QBLK)
            keys = slice(k0 * GRID_W, k0 * GRID_W + NA_KWIN)
            q = q_ref[rows, hc]
            s_loc = _dot_nt(q, k_ref[keys, hc]) * SCORE_SCALE + bias_ref[h, pat]
            s_ctx = _dot_nt(q, kc) * SCORE_SCALE
            m = jnp.maximum(jnp.max(s_loc, axis=-1, keepdims=True), jnp.max(s_ctx, axis=-1, keepdims=True))
            p_loc = jnp.exp2(s_loc - m).astype(BF16)
            p_ctx = jnp.exp2(s_ctx - m).astype(BF16)
            o_aug = _dot(p_loc, vs[keys, :]) + _dot(p_ctx, vs[seq:, :])
            o_ref[rows, hc] = _normalised(o_aug).astype(o_ref.dtype)


def _na_attention(z, zc, ctx_cols, bias, layer, plan, batch, seq, ctx_len, heads=4):
    starts, pat_ids, patterns = plan
    hw = heads * HEAD_DIM
    steps = NA_HEADS // heads
    qa, ka, va = (Z_COLS.block(c, hw) for c in (W_QA, W_KA, W_VA))
    kc, vc = (ctx_cols.block(c, hw) for c in (W_KA, W_VA))
    kernel = functools.partial(_na_kernel, starts=tuple(starts), pat_ids=tuple(pat_ids), seq=seq)
    return pl.pallas_call(
        kernel,
        grid=(steps, batch),
        in_specs=[
            pl.BlockSpec((seq, hw), lambda h, b: (b, qa(h))),
            pl.BlockSpec((seq, hw), lambda h, b: (b, ka(h))),
            pl.BlockSpec((seq, hw), lambda h, b: (b, va(h))),
            pl.BlockSpec((ctx_len, hw), lambda h, b: (b, kc(h))),
            pl.BlockSpec((ctx_len, hw), lambda h, b: (b, vc(h))),
            pl.BlockSpec((heads, len(patterns), NA_QBLK, NA_KWIN), lambda h, b: (layer * steps + h, 0, 0, 0)),
        ],
        out_specs=pl.BlockSpec((seq, hw), lambda h, b: (b, h)),
        out_shape=jax.ShapeDtypeStruct((batch * seq, BRANCH_W), BF16),
        scratch_shapes=[pltpu.VMEM((heads, seq + ctx_len, 2 * HEAD_DIM), BF16)],
        compiler_params=_params("parallel", "parallel"),
        name="na_attention",
    )(z, z, z, zc, zc, bias)


def _gqa_kernel(q_ref, k_ref, v_ref, kc_ref, vc_ref, qg_ref, kg_ref, cos_ref, sin_ref, o_ref,
                ks_ref, vs_ref, *, tq, seq, qrows):
    qi = pl.program_id(2)

    @pl.when(qi == 0)
    def _():
        kg = kg_ref[...]
        kn = _rms_norm(k_ref[...].astype(F32), kg)
        ks_ref[0:seq, :] = _rope(kn, cos_ref[...], sin_ref[...]).astype(BF16)
        ks_ref[seq:, :] = _rms_norm(kc_ref[...].astype(F32), kg).astype(BF16)
        _fill_values_with_ones(vs_ref, v_ref[...], vc_ref[...], seq)

    qg = qg_ref[...]
    for c in range(tq // qrows):
        r = pl.multiple_of(qi * tq + c * qrows, qrows)
        cos = cos_ref[pl.ds(r, qrows), :]
        sin = sin_ref[pl.ds(r, qrows), :]
        rows = slice(c * qrows, (c + 1) * qrows)
        for g in range(GQA_GROUP):
            cols = slice(g * HEAD_DIM, (g + 1) * HEAD_DIM)
            q = q_ref[rows, cols].astype(F32)
            qn = (_rope(_rms_norm(q, qg), cos, sin) * SCORE_SCALE).astype(BF16)
            s = _dot_nt(qn, ks_ref[...])
            m = jnp.max(s, axis=-1, keepdims=True)
            p = jnp.exp2(s - m).astype(BF16)
            o_ref[rows, cols] = _normalised(_dot(p, vs_ref[...])).astype(o_ref.dtype)


def _gqa_attention(z, zc, ctx_cols, q_gain, k_gain, layer, cos, sin, batch, seq, ctx_len, tq=None, qrows=256):
    hb = HEAD_DIM
    gw = GQA_GROUP * HEAD_DIM
    tq = seq if tq is None else tq
    nq = seq // tq
    qb = Z_COLS.block(W_QB, gw)
    kb, vb = (Z_COLS.block(c, hb) for c in (W_KB, W_VB))
    kc, vc = (ctx_cols.block(c, hb) for c in (W_KB, W_VB))
    kernel = functools.partial(_gqa_kernel, tq=tq, seq=seq, qrows=qrows)
    gain = pl.BlockSpec((None, 1, hb), lambda b, kh, qi: (layer, 0, 0))
    return pl.pallas_call(
        kernel,
        grid=(batch, GQA_KV_HEADS, nq),
        in_specs=[
            pl.BlockSpec((tq, gw), lambda b, kh, qi: (b * nq + qi, qb(kh))),
            pl.BlockSpec((seq, hb), lambda b, kh, qi: (b, kb(kh))),
            pl.BlockSpec((seq, hb), lambda b, kh, qi: (b, vb(kh))),
            pl.BlockSpec((ctx_len, hb), lambda b, kh, qi: (b, kc(kh))),
            pl.BlockSpec((ctx_len, hb), lambda b, kh, qi: (b, vc(kh))),
            gain, gain,
            pl.BlockSpec((seq, hb), lambda b, kh, qi: (0, 0)),
            pl.BlockSpec((seq, hb), lambda b, kh, qi: (0, 0)),
        ],
        out_specs=pl.BlockSpec((tq, gw), lambda b, kh, qi: (b * nq + qi, kh)),
        out_shape=jax.ShapeDtypeStruct((batch * seq, BRANCH_W), BF16),
        scratch_shapes=[pltpu.VMEM((seq + ctx_len, hb), BF16), pltpu.VMEM((seq + ctx_len, 2 * hb), BF16)],
        compiler_params=_params("parallel", "parallel", "arbitrary"),
        name="gqa_attention",
    )(z, z, z, zc, zc, q_gain, k_gain, cos, sin)


def _ctx_attn_kernel(q_ref, k_ref, v_ref, qg_ref, kg_ref, o_ref, *, qk_norm):
    shared = k_ref.shape[1] == HEAD_DIM
    for h in range(q_ref.shape[1] // HEAD_DIM):
        hc = slice(h * HEAD_DIM, (h + 1) * HEAD_DIM)
        kvc = slice(0, HEAD_DIM) if shared else hc
        q = q_ref[:, hc]
        k = k_ref[:, kvc]
        if qk_norm:
            q = _rms_norm(q.astype(F32), qg_ref[...]).astype(BF16)
            k = _rms_norm(k.astype(F32), kg_ref[...]).astype(BF16)
        s = _dot_nt(q, k) * ATTN_SCALE
        m = jnp.max(s, axis=-1, keepdims=True)
        p = jnp.exp(s - m)
        den = jnp.sum(p, axis=-1, keepdims=True)
        o_ref[:, hc] = (_dot(p.astype(BF16), v_ref[:, kvc]) / den).astype(o_ref.dtype)


def _ctx_attention(zc, w_q, w_k, w_v, shared_kv, q_gain, k_gain, layer, qk_norm, batch, ctx_len):
    gw = GQA_GROUP * HEAD_DIM
    kvw = HEAD_DIM if shared_kv else gw
    qc = Z_COLS.block(w_q, gw)
    kc, vc = (Z_COLS.block(c, kvw) for c in (w_k, w_v))
    kernel = functools.partial(_ctx_attn_kernel, qk_norm=qk_norm)
    gain = pl.BlockSpec((None, 1, HEAD_DIM), lambda b, g: (layer, 0, 0))
    return pl.pallas_call(
        kernel,
        grid=(batch, BRANCH_W // gw),
        in_specs=[
            pl.BlockSpec((ctx_len, gw), lambda b, g: (b, qc(g))),
            pl.BlockSpec((ctx_len, kvw), lambda b, g: (b, kc(g))),
            pl.BlockSpec((ctx_len, kvw), lambda b, g: (b, vc(g))),
            gain, gain,
        ],
        out_specs=pl.BlockSpec((ctx_len, gw), lambda b, g: (b, g)),
        out_shape=jax.ShapeDtypeStruct((batch * ctx_len, BRANCH_W), BF16),
        compiler_params=_params("parallel", "parallel"),
        name="ctx_attention",
    )(zc, zc, zc, q_gain, k_gain)


def _conv_tile(c_ref, prev_ref, next_ref, w_ref, *, seq_len, tc):
    tm = c_ref.shape[0]
    chunks = []
    r0 = pl.program_id(0) * tm
    t = (r0 + lax.broadcasted_iota(jnp.int32, (tm, 1), 0)) & (seq_len - 1)
    row = lax.broadcasted_iota(jnp.int32, (tm, 1), 0)
    at_start, at_end = t == 0, t == seq_len - 1
    w = w_ref[...]
    for j in range(BRANCH_W // tc):
        cb = slice(j * tc, (j + 1) * tc)
        cc = slice(BRANCH_W + j * tc, BRANCH_W + (j + 1) * tc)
        cx = slice(2 * BRANCH_W + j * tc, 2 * BRANCH_W + (j + 1) * tc)
        v = c_ref[:, cc].astype(F32) * c_ref[:, cx].astype(F32)
        above = prev_ref[SUBLANES - 1:SUBLANES, cc].astype(F32) * prev_ref[SUBLANES - 1:SUBLANES, cx].astype(F32)
        below = next_ref[0:1, cc].astype(F32) * next_ref[0:1, cx].astype(F32)
        prev = jnp.where(row == 0, above, pltpu.roll(v, 1, axis=0))
        nxt = jnp.where(row == tm - 1, below, pltpu.roll(v, tm - 1, axis=0))
        prev = jnp.where(at_start, 0.0, prev)
        nxt = jnp.where(at_end, 0.0, nxt)
        y = w[0:1, cb] * prev + w[1:2, cb] * v + w[2:3, cb] * nxt
        chunks.append((c_ref[:, cb].astype(F32) * y).astype(BF16))
    return jnp.concatenate(chunks, axis=1)


def _merge_kernel(a0_ref, a1_ref, c_ref, prev_ref, next_ref, cw_ref, g_ref, w_ref, o_ref, *, tn, seq_len, tc):
    d = o_ref.shape[1]
    branches = (a0_ref[...], a1_ref[...], _conv_tile(c_ref, prev_ref, next_ref, cw_ref, seq_len=seq_len, tc=tc))
    for j in range(d // tn):
        cols = slice(j * tn, (j + 1) * tn)
        acc = None
        for k, a in enumerate(branches):
            gate = jax.nn.sigmoid(g_ref[:, k * d + j * tn:k * d + (j + 1) * tn].astype(F32))
            term = gate * _dot(a, w_ref[k, :, cols])
            acc = term if acc is None else acc + term
        o_ref[:, cols] = acc.astype(o_ref.dtype)


def _merge(o_na, o_gqa, z, conv_w, w_branch, layer, seq_len, tm=512, tn=512, tc=256):
    m = o_na.shape[0]
    assert seq_len & (seq_len - 1) == 0 and (tm % seq_len == 0 or seq_len % tm == 0) and m % seq_len == 0
    conv_blk, gate_blk = Z_COLS.index(W_CB, CONV_W), Z_COLS.index(W_G, GATE_W)
    halo = tm // SUBLANES
    last_halo = m // SUBLANES - 1
    act = pl.BlockSpec((tm, BRANCH_W), lambda i: (i, 0))
    return pl.pallas_call(
        functools.partial(_merge_kernel, tn=tn, seq_len=seq_len, tc=tc),
        grid=(m // tm,),
        in_specs=[act, act,
                  pl.BlockSpec((tm, CONV_W), lambda i: (i, conv_blk)),
                  pl.BlockSpec((SUBLANES, CONV_W), lambda i: (jnp.maximum(i * halo - 1, 0), conv_blk)),
                  pl.BlockSpec((SUBLANES, CONV_W), lambda i: (jnp.minimum((i + 1) * halo, last_halo), conv_blk)),
                  pl.BlockSpec((None, CONV_K, BRANCH_W), lambda i: (layer, 0, 0)),
                  pl.BlockSpec((tm, GATE_W), lambda i: (i, gate_blk)),
                  _resident((None, N_BRANCH, BRANCH_W, D_MODEL), lambda i: (layer, 0, 0, 0))],
        out_specs=pl.BlockSpec((tm, D_MODEL), lambda i: (i, 0)),
        out_shape=jax.ShapeDtypeStruct((m, D_MODEL), BF16),
        compiler_params=_params("parallel"),
        name="merge",
    )(o_na, o_gqa, z, z, z, conv_w, z, w_branch)


def _out_proj_kernel(m_ref, w_ref, x_ref, gate_ref, g_ref, b_ref, o_ref, *, alpha, rows):
    gate = gate_ref[...]
    g = g_ref[...]
    b = b_ref[...]
    for c in range(o_ref.shape[0] // rows):
        f = _dot(m_ref[c * rows:(c + 1) * rows, :], w_ref[...])
        _residual_norm_rows(x_ref, f, gate, g, b, o_ref, c * rows, alpha)


def _out_proj(m_act, w_o, x, mod3, layer, gate_blk, ln_g, ln_b, ln_row, alpha, rows_per_mod, fixed_row,
              tm=512, rows=256):
    m, d = x.shape
    row = _mod_row_map(layer, rows_per_mod, tm, fixed_row)
    vec = pl.BlockSpec((None, 1, d), lambda i: (ln_row, 0, 0))
    return pl.pallas_call(
        functools.partial(_out_proj_kernel, alpha=alpha, rows=rows),
        grid=(m // tm,),
        in_specs=[
            pl.BlockSpec((tm, d), lambda i: (i, 0)),
            _resident((None, d, d), lambda i: (layer, 0, 0)),
            pl.BlockSpec((tm, d), lambda i: (i, 0)),
            pl.BlockSpec((None, 1, d), lambda i: (row(i), 0, gate_blk)),
            vec, vec,
        ],
        out_specs=pl.BlockSpec((tm, d), lambda i: (i, 0)),
        out_shape=jax.ShapeDtypeStruct((m, d), F32),
        compiler_params=_params("parallel"),
        name="out_proj",
    )(m_act, w_o, x, mod3, ln_g, ln_b)


def _mlp_kernel(x_ref, sh_ref, sc_ref, gate_ref, g_ref, b_ref, wu_ref, wd_ref, o_ref, h_ref, *, alpha, rows):
    j = pl.program_id(1)
    last = pl.num_programs(1) - 1
    n_chunks = o_ref.shape[0] // rows

    def down(r):
        u = jnp.maximum(_dot(h_ref[r, :], wu_ref[...]), 0.0)
        return _dot((u * u).astype(BF16), wd_ref[...])

    @pl.when(j == 0)
    def _():
        sc1 = 1.0 + sc_ref[...]
        sh = sh_ref[...]
        for c in range(n_chunks):
            _modulate_rows(x_ref, sh, sc1, h_ref, c * rows, rows)
            r = slice(c * rows, (c + 1) * rows)
            o_ref[r, :] = down(r)

    @pl.when(jnp.logical_and(j > 0, j < last))
    def _():
        o_ref[...] += down(slice(None))

    @pl.when(j == last)
    def _():
        gate = gate_ref[...]
        g = g_ref[...]
        b = b_ref[...]
        for c in range(n_chunks):
            r = slice(c * rows, (c + 1) * rows)
            _residual_norm_rows(x_ref, o_ref[r, :] + down(r), gate, g, b, o_ref, c * rows, alpha)


def _mlp(x, mod3, layer, shift_blk, scale_blk, gate_blk, ln_g, ln_b, ln_row, w_up, w_down, alpha,
         rows_per_mod, fixed_row, tm=1024, th=512, rows=256):
    m, d = x.shape
    hid = w_up.shape[2]
    assert hid // th >= 2
    row = _mod_row_map(layer, rows_per_mod, tm, fixed_row)
    mod_spec = lambda blk: pl.BlockSpec((None, 1, d), lambda i, j: (row(i), 0, blk))
    vec = pl.BlockSpec((None, 1, d), lambda i, j: (ln_row, 0, 0))
    return pl.pallas_call(
        functools.partial(_mlp_kernel, alpha=alpha, rows=rows),
        grid=(m // tm, hid // th),
        in_specs=[
            pl.BlockSpec((tm, d), lambda i, j: (i, 0)),
            mod_spec(shift_blk), mod_spec(scale_blk), mod_spec(gate_blk),
            vec, vec,
            pl.BlockSpec((None, d, th), lambda i, j: (layer, 0, j)),
            pl.BlockSpec((None, th, d), lambda i, j: (layer, j, 0)),
        ],
        out_specs=pl.BlockSpec((tm, d), lambda i, j: (i, 0)),
        out_shape=jax.ShapeDtypeStruct((m, d), F32),
        scratch_shapes=[pltpu.VMEM((tm, d), BF16)],
        compiler_params=_params("parallel", "arbitrary"),
        name="mlp",
    )(x, mod3, mod3, mod3, ln_g, ln_b, w_up, w_down)


def _rope_tables(seq):
    t = jnp.arange(seq)
    row = (t // GRID_W).astype(F32)
    col = (t % GRID_W).astype(F32)
    axis_dim = HEAD_DIM // 2
    freqs = ROPE_THETA ** (-jnp.arange(0, axis_dim, 2, dtype=F32) / axis_dim)
    ar, ac = row[:, None] * freqs, col[:, None] * freqs
    cos = jnp.concatenate([jnp.cos(ar), jnp.cos(ar), jnp.cos(ac), jnp.cos(ac)], axis=-1)
    sin = jnp.concatenate([-jnp.sin(ar), jnp.sin(ar), -jnp.sin(ac), jnp.sin(ac)], axis=-1)
    return cos, sin


def kernel(x, c, ctx, c_ctx, w_mod, b_mod, w_in, rpb, q_gain, k_gain, conv_w, w_branch, w_o, w_up, w_down,
           ln_g, ln_b):
    batch, seq, d = x.shape
    ctx_len = ctx.shape[1]
    depth = w_mod.shape[0]
    assert d == D_MODEL and batch < MOD_ROWS and seq % NA_QBLK == 0 and seq // GRID_W >= NA_K_ROWS
    alpha = float((2 * depth) ** 0.25)
    ctx_row = batch

    c_all = jnp.concatenate([c, c_ctx[None, :], jnp.zeros((MOD_ROWS - batch - 1, d), F32)], axis=0)
    mod3 = _modulation(c_all, w_mod, b_mod).reshape(depth * MOD_ROWS, 1, 6 * d)
    cos, sin = _rope_tables(seq)
    plan = _na_plan(seq // GRID_W)
    bias = _na_bias(rpb.reshape((depth * NA_HEADS,) + rpb.shape[2:]), plan[2])

    w_in_b, wb_b, wo_b, wu_b, wd_b = (w.astype(BF16) for w in (w_in, w_branch, w_o, w_up, w_down))
    qg = q_gain.reshape(depth, 1, HEAD_DIM)
    kg = k_gain.reshape(depth, 1, HEAD_DIM)
    lg = ln_g.reshape(depth * 2, 1, d)
    lb = ln_b.reshape(depth * 2, 1, d)

    xl = x.reshape(batch * seq, d)
    xc = ctx.reshape(batch * ctx_len, d)
    for l in range(depth):
        with_ctx_out = l < depth - 1

        z = _ln_matmul(xl, mod3, l, 0, 1, w_in_b, Z_ROT, IN_W, seq, None, tm=1024, tn=1536)
        if with_ctx_out:
            zc = _ln_matmul(xc, mod3, l, 0, 1, w_in_b, Z_ROT, IN_W, None, ctx_row, tm=512, tn=1536)
            ctx_cols = Z_COLS
        else:
            zc = _ln_matmul(xc, mod3, l, 0, 1, w_in_b, KV_COLS.start, KV_COLS.width, None, ctx_row, tm=512, tn=512)
            ctx_cols = KV_COLS

        o_na = _na_attention(z, zc, ctx_cols, bias, l, plan, batch, seq, ctx_len)
        o_gqa = _gqa_attention(z, zc, ctx_cols, qg, kg, l, cos, sin, batch, seq, ctx_len)
        m_act = _merge(o_na, o_gqa, z, conv_w, wb_b, l, seq)
        xl = _out_proj(m_act, wo_b, xl, mod3, l, 2, lg, lb, 2 * l, alpha, seq, None)
        xl = _mlp(xl, mod3, l, 3, 4, 5, lg, lb, 2 * l + 1, wu_b, wd_b, alpha, seq, None)

        if with_ctx_out:
            o_na_c = _ctx_attention(zc, W_QA, W_KA, W_VA, False, qg, kg, l, False, batch, ctx_len)
            o_gqa_c = _ctx_attention(zc, W_QB, W_KB, W_VB, True, qg, kg, l, True, batch, ctx_len)
            m_c = _merge(o_na_c, o_gqa_c, zc, conv_w, wb_b, l, ctx_len)
            xc = _out_proj(m_c, wo_b, xc, mod3, l, 2, lg, lb, 2 * l, alpha, None, ctx_row)
            xc = _mlp(xc, mod3, l, 3, 4, 5, lg, lb, 2 * l + 1, wu_b, wd_b, alpha, None, ctx_row)
    return xl.reshape(batch, seq, d)
```

```python
import functools

import numpy as np
import jax
import jax.numpy as jnp
from jax import lax
from jax.experimental import pallas as pl
from jax.experimental.pallas import tpu as pltpu

D_MODEL = 2048
GRID_W = 64
HEAD_DIM = 128
BRANCH_W = D_MODEL // 2
NA_HEADS = BRANCH_W // HEAD_DIM
GQA_HEADS = BRANCH_W // HEAD_DIM
GQA_KV_HEADS = GQA_HEADS // 4
GQA_GROUP = GQA_HEADS // GQA_KV_HEADS
KV_W = GQA_KV_HEADS * HEAD_DIM
CONV_K = 3
MLP_HIDDEN = 4 * D_MODEL
N_BRANCH = 3
WIN_R = 8
WIN_C = 16
ROPE_THETA = 10000.0
NORM_EPS = 1e-6
NEG_INF = -1e30
ATTN_SCALE = HEAD_DIM ** -0.5
LOG2E = 1.4426950408889634
SCORE_SCALE = ATTN_SCALE * LOG2E

W_QA = 0
W_QB = W_QA + BRANCH_W
W_KA = W_QB + BRANCH_W
W_VA = W_KA + BRANCH_W
W_KB = W_VA + BRANCH_W
W_VB = W_KB + KV_W
W_CB = W_VB + KV_W
W_CC = W_CB + BRANCH_W
W_CX = W_CC + BRANCH_W
W_G = W_CX + BRANCH_W
IN_W = W_G + N_BRANCH * D_MODEL
CONV_W = 3 * BRANCH_W
GATE_W = N_BRANCH * D_MODEL
Z_ROT = 1536
W_IN_TILE = 1536
MLP_TILE = 1024


class _Cols:
    def __init__(self, start, width, rot):
        self.start, self.width, self.rot = start, width, rot

    def block(self, w_col, blk):
        off = (w_col - self.start - self.rot) % self.width
        assert off % blk == 0 and self.width % blk == 0
        n, base = self.width // blk, off // blk
        return lambda k: (base + k) % n

    def index(self, w_col, blk):
        off = (w_col - self.start - self.rot) % self.width
        assert off % blk == 0 and off + blk <= self.width
        return off // blk


Z_COLS = _Cols(0, IN_W, Z_ROT)
KV_COLS = _Cols(W_KA, W_CB - W_KA, 0)

MOD_ROWS = 32
NA_Q_ROWS = 4
NA_K_ROWS = 12
NA_QBLK = NA_Q_ROWS * GRID_W
NA_KWIN = NA_K_ROWS * GRID_W
SUBLANES = 8

VMEM_LIMIT_BYTES = 56 * 1024 * 1024
LN_CHUNK = 128

F32 = jnp.float32
BF16 = jnp.bfloat16


def _params(*semantics):
    return pltpu.CompilerParams(dimension_semantics=semantics, vmem_limit_bytes=VMEM_LIMIT_BYTES)


def _resident(block_shape, index_map):
    return pl.BlockSpec(block_shape, index_map, pipeline_mode=pl.Buffered(1))


def _layer_norm(x, eps=NORM_EPS):
    mu = jnp.mean(x, axis=-1, keepdims=True)
    xc = x - mu
    var = jnp.mean(xc * xc, axis=-1, keepdims=True)
    return xc * lax.rsqrt(var + eps)


def _rms_norm(x, gain):
    return x * lax.rsqrt(jnp.mean(x * x, axis=-1, keepdims=True) + NORM_EPS) * gain


def _rope(x, cos, sin_signed):
    lane = lax.broadcasted_iota(jnp.int32, x.shape, 1)
    partner = jnp.where((lane & 32) == 0, pltpu.roll(x, 96, axis=1), pltpu.roll(x, 32, axis=1))
    return x * cos + partner * sin_signed


def _dot(a, b):
    return jnp.dot(a, b, preferred_element_type=F32)


def _dot_nt(a, b):
    return lax.dot_general(a, b, (((1,), (1,)), ((), ())), preferred_element_type=F32)


def _modulate_rows(x_ref, sh, sc1, h_ref, r0, rows):
    for s in range(rows // LN_CHUNK):
        rr = slice(r0 + s * LN_CHUNK, r0 + (s + 1) * LN_CHUNK)
        h_ref[rr, :] = (_layer_norm(x_ref[rr, :]) * sc1 + sh).astype(h_ref.dtype)


def _residual_norm_rows(x_ref, f, gate, g, b, o_ref, r0, alpha):
    gate_a = gate * (1.0 / alpha)
    eps_a = NORM_EPS / (alpha * alpha)
    for s in range(f.shape[0] // LN_CHUNK):
        rr = slice(r0 + s * LN_CHUNK, r0 + (s + 1) * LN_CHUNK)
        y = x_ref[rr, :] + gate_a * f[s * LN_CHUNK:(s + 1) * LN_CHUNK, :]
        o_ref[rr, :] = _layer_norm(y, eps_a) * g + b


def _mod_row_map(layer, rows_per_mod, tm, fixed_row):
    base = layer * MOD_ROWS
    if fixed_row is not None:
        return lambda i: base + fixed_row
    return lambda i: base + (i * tm) // rows_per_mod


def _mod_kernel(c_ref, w_ref, b_ref, o_ref):
    c = c_ref[...]
    a = (c * jax.nn.sigmoid(c)).astype(BF16)
    o_ref[...] = _dot(a, w_ref[...].astype(BF16)) + b_ref[...]


def _modulation(c_all, w_mod, b_mod, tn=1024):
    depth, d, n = w_mod.shape
    return pl.pallas_call(
        _mod_kernel,
        grid=(depth, n // tn),
        in_specs=[
            pl.BlockSpec((MOD_ROWS, d), lambda l, j: (0, 0)),
            pl.BlockSpec((None, d, tn), lambda l, j: (l, 0, j)),
            pl.BlockSpec((None, 1, tn), lambda l, j: (l, 0, j)),
        ],
        out_specs=pl.BlockSpec((None, MOD_ROWS, tn), lambda l, j: (l, 0, j)),
        out_shape=jax.ShapeDtypeStruct((depth, MOD_ROWS, n), F32),
        compiler_params=_params("parallel", "parallel"),
        name="modulation",
    )(c_all, w_mod, b_mod.reshape(depth, 1, n))


def _ln_matmul_kernel(x_ref, sh_ref, sc_ref, w_ref, o_ref, h_ref, *, rows):
    j = pl.program_id(1)

    @pl.when(j == 0)
    def _():
        sc1 = 1.0 + sc_ref[...]
        sh = sh_ref[...]
        for c in range(x_ref.shape[0] // rows):
            _modulate_rows(x_ref, sh, sc1, h_ref, c * rows, rows)
            r = slice(c * rows, (c + 1) * rows)
            o_ref[r, :] = _dot(h_ref[r, :], w_ref[...]).astype(o_ref.dtype)

    @pl.when(j > 0)
    def _():
        o_ref[...] = _dot(h_ref[...], w_ref[...]).astype(o_ref.dtype)


def _ln_matmul(x, mod3, layer, shift_blk, scale_blk, w, first_col, n, rows_per_mod, fixed_row, tm, tn, rows=256):
    m, d = x.shape
    w_tiles, tw = w.shape[1], w.shape[3]
    per_tile = tw // tn
    assert first_col % tn == 0 and n % tn == 0 and tw % tn == 0
    row = _mod_row_map(layer, rows_per_mod, tm, fixed_row)

    def w_index(i, j):
        c = first_col // tn + j
        return layer, (c // per_tile) % w_tiles, 0, c % per_tile

    return pl.pallas_call(
        functools.partial(_ln_matmul_kernel, rows=rows),
        grid=(m // tm, n // tn),
        in_specs=[
            pl.BlockSpec((tm, d), lambda i, j: (i, 0)),
            pl.BlockSpec((None, 1, d), lambda i, j: (row(i), 0, shift_blk)),
            pl.BlockSpec((None, 1, d), lambda i, j: (row(i), 0, scale_blk)),
            pl.BlockSpec((None, None, d, tn), w_index),
        ],
        out_specs=pl.BlockSpec((tm, tn), lambda i, j: (i, j)),
        out_shape=jax.ShapeDtypeStruct((m, n), BF16),
        scratch_shapes=[pltpu.VMEM((tm, d), BF16)],
        compiler_params=_params("parallel", "arbitrary"),
        name="ln_matmul",
    )(x, mod3, mod3, w)


def _na_plan(n_rows):
    wr = min(WIN_R, n_rows)
    n_qblk = n_rows // NA_Q_ROWS
    r = np.arange(n_rows)
    row_start = np.clip(r - wr // 2, 0, n_rows - wr)
    starts, pat_ids, patterns, keys = [], [], [], {}
    for qi in range(n_qblk):
        k0 = int(np.clip(qi * NA_Q_ROWS - NA_Q_ROWS, 0, n_rows - NA_K_ROWS))
        q_rows = qi * NA_Q_ROWS + np.arange(NA_Q_ROWS)
        k_rows = k0 + np.arange(NA_K_ROWS)
        valid = (k_rows[None, :] >= row_start[q_rows][:, None]) & (k_rows[None, :] < row_start[q_rows][:, None] + wr)
        assert valid.sum(axis=1).min() == wr, "key window must cover every query row's neighbourhood"
        dr_idx = np.clip(k_rows[None, :] - q_rows[:, None] + WIN_R - 1, 0, 2 * WIN_R - 2)
        key = (valid.tobytes(), dr_idx.tobytes())
        if key not in keys:
            keys[key] = len(patterns)
            patterns.append((valid, dr_idx))
        starts.append(k0)
        pat_ids.append(keys[key])
    return starts, pat_ids, patterns


def _na_bias(rpb, patterns):
    n, n_dr, n_dc = rpb.shape
    col = np.arange(GRID_W)
    col_start = np.clip(col - WIN_C // 2, 0, GRID_W - WIN_C)
    in_win = (col[None, :] >= col_start[:, None]) & (col[None, :] < col_start[:, None] + WIN_C)
    assert np.abs(col[None, :] - col[:, None])[in_win].max() <= WIN_C - 1
    line = 2 * GRID_W
    left = GRID_W - 1 - (WIN_C - 1)
    w = jnp.pad(rpb.astype(F32), ((0, 0), (0, 0), (left, line - left - n_dc)))
    w = jnp.broadcast_to(w[:, :, None, :], (n, n_dr, GRID_W, line)).reshape(n, n_dr, GRID_W * line)
    toe = w[:, :, :GRID_W * (line - 1)].reshape(n, n_dr, GRID_W, line - 1)[..., GRID_W - 1:]
    toe = jnp.where(in_win[None, None], toe * LOG2E, NEG_INF)
    toe = toe.transpose(0, 2, 1, 3).reshape(n, GRID_W, n_dr * GRID_W)
    out = []
    for valid, dr_idx in patterns:
        per_row = []
        for qr in range(NA_Q_ROWS):
            kr = np.flatnonzero(valid[qr])
            a, b = int(kr[0]), int(kr[-1]) + 1
            d0 = int(dr_idx[qr, a])
            assert b - a == len(kr) and (dr_idx[qr, a:b] == d0 + np.arange(b - a)).all()
            piece = toe[:, :, d0 * GRID_W:(d0 + b - a) * GRID_W]
            per_row.append(jnp.pad(piece, ((0, 0), (0, 0), (a * GRID_W, (NA_K_ROWS - b) * GRID_W)),
                                   constant_values=NEG_INF))
        out.append(jnp.stack(per_row, axis=1).reshape(n, NA_QBLK, NA_KWIN))
    return jnp.stack(out, axis=1)


def _fill_values_with_ones(vs_ref, v, vc, seq):
    vs_ref[0:seq, 0:HEAD_DIM] = v
    vs_ref[seq:, 0:HEAD_DIM] = vc
    vs_ref[:, HEAD_DIM:] = jnp.ones((vs_ref.shape[0], HEAD_DIM), vs_ref.dtype)


def _normalised(o_aug):
    return o_aug[:, :HEAD_DIM] / o_aug[:, HEAD_DIM:]


def _na_kernel(q_ref, k_ref, v_ref, kc_ref, vc_ref, bias_ref, o_ref, vs_ref, *, starts, pat_ids, seq):
    for h in range(q_ref.shape[1] // HEAD_DIM):
        hc = slice(h * HEAD_DIM, (h + 1) * HEAD_DIM)
        vs = vs_ref.at[h]
        _fill_values_with_ones(vs, v_ref[:, hc], vc_ref[:, hc], seq)
        kc = kc_ref[:, hc]
        for qi, (k0, pat) in enumerate(zip(starts, pat_ids)):
            rows = slice(qi * NA_QBLK, (qi + 1) * NA_QBLK)
            keys = slice(k0 * GRID_W, k0 * GRID_W + NA_KWIN)
            q = q_ref[rows, hc]
            s_loc = _dot_nt(q, k_ref[keys, hc]) * SCORE_SCALE + bias_ref[h, pat]
            s_ctx = _dot_nt(q, kc) * SCORE_SCALE
            m = jnp.maximum(jnp.max(s_loc, axis=-1, keepdims=True), jnp.max(s_ctx, axis=-1, keepdims=True))
            p_loc = jnp.exp2(s_loc - m).astype(BF16)
            p_ctx = jnp.exp2(s_ctx - m).astype(BF16)
            o_aug = _dot(p_loc, vs[keys, :]) + _dot(p_ctx, vs[seq:, :])
            o_ref[rows, hc] = _normalised(o_aug).astype(o_ref.dtype)


def _na_attention(z, zc, ctx_cols, bias, layer, plan, batch, seq, ctx_len, heads=4):
    starts, pat_ids, patterns = plan
    hw = heads * HEAD_DIM
    steps = NA_HEADS // heads
    qa, ka, va = (Z_COLS.block(c, hw) for c in (W_QA, W_KA, W_VA))
    kc, vc = (ctx_cols.block(c, hw) for c in (W_KA, W_VA))
    kernel = functools.partial(_na_kernel, starts=tuple(starts), pat_ids=tuple(pat_ids), seq=seq)
    return pl.pallas_call(
        kernel,
        grid=(steps, batch),
        in_specs=[
            pl.BlockSpec((seq, hw), lambda h, b: (b, qa(h))),
            pl.BlockSpec((seq, hw), lambda h, b: (b, ka(h))),
            pl.BlockSpec((seq, hw), lambda h, b: (b, va(h))),
            pl.BlockSpec((ctx_len, hw), lambda h, b: (b, kc(h))),
            pl.BlockSpec((ctx_len, hw), lambda h, b: (b, vc(h))),
            pl.BlockSpec((heads, len(patterns), NA_QBLK, NA_KWIN), lambda h, b: (layer * steps + h, 0, 0, 0)),
        ],
        out_specs=pl.BlockSpec((seq, hw), lambda h, b: (b, h)),
        out_shape=jax.ShapeDtypeStruct((batch * seq, BRANCH_W), BF16),
        scratch_shapes=[pltpu.VMEM((heads, seq + ctx_len, 2 * HEAD_DIM), BF16)],
        compiler_params=_params("parallel", "parallel"),
        name="na_attention",
    )(z, z, z, zc, zc, bias)


def _gqa_kernel(q_ref, k_ref, v_ref, kc_ref, vc_ref, qg_ref, kg_ref, cos_ref, sin_ref, o_ref,
                ks_ref, vs_ref, *, tq, seq, qrows):
    qi = pl.program_id(2)

    @pl.when(qi == 0)
    def _():
        kg = kg_ref[...]
        kn = _rms_norm(k_ref[...].astype(F32), kg)
        ks_ref[0:seq, :] = _rope(kn, cos_ref[...], sin_ref[...]).astype(BF16)
        ks_ref[seq:, :] = _rms_norm(kc_ref[...].astype(F32), kg).astype(BF16)
        _fill_values_with_ones(vs_ref, v_ref[...], vc_ref[...], seq)

    qg = qg_ref[...]
    for c in range(tq // qrows):
        r = pl.multiple_of(qi * tq + c * qrows, qrows)
        cos = cos_ref[pl.ds(r, qrows), :]
        sin = sin_ref[pl.ds(r, qrows), :]
        rows = slice(c * qrows, (c + 1) * qrows)
        for g in range(GQA_GROUP):
            cols = slice(g * HEAD_DIM, (g + 1) * HEAD_DIM)
            q = q_ref[rows, cols].astype(F32)
            qn = (_rope(_rms_norm(q, qg), cos, sin) * SCORE_SCALE).astype(BF16)
            s = _dot_nt(qn, ks_ref[...])
            m = jnp.max(s, axis=-1, keepdims=True)
            p = jnp.exp2(s - m).astype(BF16)
            o_ref[rows, cols] = _normalised(_dot(p, vs_ref[...])).astype(o_ref.dtype)


def _gqa_attention(z, zc, ctx_cols, q_gain, k_gain, layer, cos, sin, batch, seq, ctx_len, tq=None, qrows=256):
    hb = HEAD_DIM
    gw = GQA_GROUP * HEAD_DIM
    tq = seq if tq is None else tq
    nq = seq // tq
    qb = Z_COLS.block(W_QB, gw)
    kb, vb = (Z_COLS.block(c, hb) for c in (W_KB, W_VB))
    kc, vc = (ctx_cols.block(c, hb) for c in (W_KB, W_VB))
    kernel = functools.partial(_gqa_kernel, tq=tq, seq=seq, qrows=qrows)
    gain = pl.BlockSpec((None, 1, hb), lambda b, kh, qi: (layer, 0, 0))
    return pl.pallas_call(
        kernel,
        grid=(batch, GQA_KV_HEADS, nq),
        in_specs=[
            pl.BlockSpec((tq, gw), lambda b, kh, qi: (b * nq + qi, qb(kh))),
            pl.BlockSpec((seq, hb), lambda b, kh, qi: (b, kb(kh))),
            pl.BlockSpec((seq, hb), lambda b, kh, qi: (b, vb(kh))),
            pl.BlockSpec((ctx_len, hb), lambda b, kh, qi: (b, kc(kh))),
            pl.BlockSpec((ctx_len, hb), lambda b, kh, qi: (b, vc(kh))),
            gain, gain,
            pl.BlockSpec((seq, hb), lambda b, kh, qi: (0, 0)),
            pl.BlockSpec((seq, hb), lambda b, kh, qi: (0, 0)),
        ],
        out_specs=pl.BlockSpec((tq, gw), lambda b, kh, qi: (b * nq + qi, kh)),
        out_shape=jax.ShapeDtypeStruct((batch * seq, BRANCH_W), BF16),
        scratch_shapes=[pltpu.VMEM((seq + ctx_len, hb), BF16), pltpu.VMEM((seq + ctx_len, 2 * hb), BF16)],
        compiler_params=_params("parallel", "parallel", "arbitrary"),
        name="gqa_attention",
    )(z, z, z, zc, zc, q_gain, k_gain, cos, sin)


def _ctx_attn_kernel(q_ref, k_ref, v_ref, qg_ref, kg_ref, o_ref, *, qk_norm):
    shared = k_ref.shape[1] == HEAD_DIM
    for h in range(q_ref.shape[1] // HEAD_DIM):
        hc = slice(h * HEAD_DIM, (h + 1) * HEAD_DIM)
        kvc = slice(0, HEAD_DIM) if shared else hc
        q = q_ref[:, hc]
        k = k_ref[:, kvc]
        if qk_norm:
            q = _rms_norm(q.astype(F32), qg_ref[...]).astype(BF16)
            k = _rms_norm(k.astype(F32), kg_ref[...]).astype(BF16)
        s = _dot_nt(q, k) * ATTN_SCALE
        m = jnp.max(s, axis=-1, keepdims=True)
        p = jnp.exp(s - m)
        den = jnp.sum(p, axis=-1, keepdims=True)
        o_ref[:, hc] = (_dot(p.astype(BF16), v_ref[:, kvc]) / den).astype(o_ref.dtype)


def _ctx_attention(zc, w_q, w_k, w_v, shared_kv, q_gain, k_gain, layer, qk_norm, batch, ctx_len):
    gw = GQA_GROUP * HEAD_DIM
    kvw = HEAD_DIM if shared_kv else gw
    qc = Z_COLS.block(w_q, gw)
    kc, vc = (Z_COLS.block(c, kvw) for c in (w_k, w_v))
    kernel = functools.partial(_ctx_attn_kernel, qk_norm=qk_norm)
    gain = pl.BlockSpec((None, 1, HEAD_DIM), lambda b, g: (layer, 0, 0))
    return pl.pallas_call(
        kernel,
        grid=(batch, BRANCH_W // gw),
        in_specs=[
            pl.BlockSpec((ctx_len, gw), lambda b, g: (b, qc(g))),
            pl.BlockSpec((ctx_len, kvw), lambda b, g: (b, kc(g))),
            pl.BlockSpec((ctx_len, kvw), lambda b, g: (b, vc(g))),
            gain, gain,
        ],
        out_specs=pl.BlockSpec((ctx_len, gw), lambda b, g: (b, g)),
        out_shape=jax.ShapeDtypeStruct((batch * ctx_len, BRANCH_W), BF16),
        compiler_params=_params("parallel", "parallel"),
        name="ctx_attention",
    )(zc, zc, zc, q_gain, k_gain)


def _conv_tile(c_ref, prev_ref, next_ref, w_ref, *, seq_len, tc):
    tm = c_ref.shape[0]
    chunks = []
    r0 = pl.program_id(0) * tm
    t = (r0 + lax.broadcasted_iota(jnp.int32, (tm, 1), 0)) & (seq_len - 1)
    row = lax.broadcasted_iota(jnp.int32, (tm, 1), 0)
    at_start, at_end = t == 0, t == seq_len - 1
    w = w_ref[...]
    for j in range(BRANCH_W // tc):
        cb = slice(j * tc, (j + 1) * tc)
        cc = slice(BRANCH_W + j * tc, BRANCH_W + (j + 1) * tc)
        cx = slice(2 * BRANCH_W + j * tc, 2 * BRANCH_W + (j + 1) * tc)
        v = c_ref[:, cc].astype(F32) * c_ref[:, cx].astype(F32)
        above = prev_ref[SUBLANES - 1:SUBLANES, cc].astype(F32) * prev_ref[SUBLANES - 1:SUBLANES, cx].astype(F32)
        below = next_ref[0:1, cc].astype(F32) * next_ref[0:1, cx].astype(F32)
        prev = jnp.where(row == 0, above, pltpu.roll(v, 1, axis=0))
        nxt = jnp.where(row == tm - 1, below, pltpu.roll(v, tm - 1, axis=0))
        prev = jnp.where(at_start, 0.0, prev)
        nxt = jnp.where(at_end, 0.0, nxt)
        y = w[0:1, cb] * prev + w[1:2, cb] * v + w[2:3, cb] * nxt
        chunks.append((c_ref[:, cb].astype(F32) * y).astype(BF16))
    return jnp.concatenate(chunks, axis=1)


def _merge_kernel(a0_ref, a1_ref, c_ref, prev_ref, next_ref, cw_ref, g_ref, w_ref, o_ref, *, tn, seq_len, tc):
    d = o_ref.shape[1]
    branches = (a0_ref[...], a1_ref[...], _conv_tile(c_ref, prev_ref, next_ref, cw_ref, seq_len=seq_len, tc=tc))
    for j in range(d // tn):
        cols = slice(j * tn, (j + 1) * tn)
        acc = None
        for k, a in enumerate(branches):
            gate = jax.nn.sigmoid(g_ref[:, k * d + j * tn:k * d + (j + 1) * tn].astype(F32))
            term = gate * _dot(a, w_ref[k, :, cols])
            acc = term if acc is None else acc + term
        o_ref[:, cols] = acc.astype(o_ref.dtype)


def _merge(o_na, o_gqa, z, conv_w, w_branch, layer, seq_len, tm=512, tn=512, tc=256):
    m = o_na.shape[0]
    assert seq_len & (seq_len - 1) == 0 and (tm % seq_len == 0 or seq_len % tm == 0) and m % seq_len == 0
    conv_blk, gate_blk = Z_COLS.index(W_CB, CONV_W), Z_COLS.index(W_G, GATE_W)
    halo = tm // SUBLANES
    last_halo = m // SUBLANES - 1
    act = pl.BlockSpec((tm, BRANCH_W), lambda i: (i, 0))
    return pl.pallas_call(
        functools.partial(_merge_kernel, tn=tn, seq_len=seq_len, tc=tc),
        grid=(m // tm,),
        in_specs=[act, act,
                  pl.BlockSpec((tm, CONV_W), lambda i: (i, conv_blk)),
                  pl.BlockSpec((SUBLANES, CONV_W), lambda i: (jnp.maximum(i * halo - 1, 0), conv_blk)),
                  pl.BlockSpec((SUBLANES, CONV_W), lambda i: (jnp.minimum((i + 1) * halo, last_halo), conv_blk)),
                  pl.BlockSpec((None, CONV_K, BRANCH_W), lambda i: (layer, 0, 0)),
                  pl.BlockSpec((tm, GATE_W), lambda i: (i, gate_blk)),
                  _resident((None, N_BRANCH, BRANCH_W, D_MODEL), lambda i: (layer, 0, 0, 0))],
        out_specs=pl.BlockSpec((tm, D_MODEL), lambda i: (i, 0)),
        out_shape=jax.ShapeDtypeStruct((m, D_MODEL), BF16),
        compiler_params=_params("parallel"),
        name="merge",
    )(o_na, o_gqa, z, z, z, conv_w, z, w_branch)


def _out_proj_kernel(m_ref, w_ref, x_ref, gate_ref, g_ref, b_ref, o_ref, *, alpha, rows):
    gate = gate_ref[...]
    g = g_ref[...]
    b = b_ref[...]
    for c in range(o_ref.shape[0] // rows):
        f = _dot(m_ref[c * rows:(c + 1) * rows, :], w_ref[...])
        _residual_norm_rows(x_ref, f, gate, g, b, o_ref, c * rows, alpha)


def _out_proj(m_act, w_o, x, mod3, layer, gate_blk, ln_g, ln_b, ln_row, alpha, rows_per_mod, fixed_row,
              tm=512, rows=256):
    m, d = x.shape
    row = _mod_row_map(layer, rows_per_mod, tm, fixed_row)
    vec = pl.BlockSpec((None, 1, d), lambda i: (ln_row, 0, 0))
    return pl.pallas_call(
        functools.partial(_out_proj_kernel, alpha=alpha, rows=rows),
        grid=(m // tm,),
        in_specs=[
            pl.BlockSpec((tm, d), lambda i: (i, 0)),
            _resident((None, d, d), lambda i: (layer, 0, 0)),
            pl.BlockSpec((tm, d), lambda i: (i, 0)),
            pl.BlockSpec((None, 1, d), lambda i: (row(i), 0, gate_blk)),
            vec, vec,
        ],
        out_specs=pl.BlockSpec((tm, d), lambda i: (i, 0)),
        out_shape=jax.ShapeDtypeStruct((m, d), F32),
        compiler_params=_params("parallel"),
        name="out_proj",
    )(m_act, w_o, x, mod3, ln_g, ln_b)


def _mlp_kernel(x_ref, sh_ref, sc_ref, gate_ref, g_ref, b_ref, wu_ref, wd_ref, o_ref, h_ref, *, alpha, rows):
    j = pl.program_id(1)
    last = pl.num_programs(1) - 1
    n_chunks = o_ref.shape[0] // rows

    def down(r):
        u = jnp.maximum(_dot(h_ref[r, :], wu_ref[...]), 0.0)
        return _dot((u * u).astype(BF16), wd_ref[...])

    @pl.when(j == 0)
    def _():
        sc1 = 1.0 + sc_ref[...]
        sh = sh_ref[...]
        for c in range(n_chunks):
            _modulate_rows(x_ref, sh, sc1, h_ref, c * rows, rows)
            r = slice(c * rows, (c + 1) * rows)
            o_ref[r, :] = down(r)

    @pl.when(jnp.logical_and(j > 0, j < last))
    def _():
        o_ref[...] += down(slice(None))

    @pl.when(j == last)
    def _():
        gate = gate_ref[...]
        g = g_ref[...]
        b = b_ref[...]
        for c in range(n_chunks):
            r = slice(c * rows, (c + 1) * rows)
            _residual_norm_rows(x_ref, o_ref[r, :] + down(r), gate, g, b, o_ref, c * rows, alpha)


def _mlp(x, mod3, layer, shift_blk, scale_blk, gate_blk, ln_g, ln_b, ln_row, w_up, w_down, alpha,
         rows_per_mod, fixed_row, tm=512, rows=256):
    m, d = x.shape
    n_tiles, th = w_up.shape[1], w_up.shape[3]
    assert n_tiles >= 2 and w_down.shape[1] == n_tiles * th
    row = _mod_row_map(layer, rows_per_mod, tm, fixed_row)
    mod_spec = lambda blk: pl.BlockSpec((None, 1, d), lambda i, j: (row(i), 0, blk))
    vec = pl.BlockSpec((None, 1, d), lambda i, j: (ln_row, 0, 0))
    return pl.pallas_call(
        functools.partial(_mlp_kernel, alpha=alpha, rows=rows),
        grid=(m // tm, n_tiles),
        in_specs=[
            pl.BlockSpec((tm, d), lambda i, j: (i, 0)),
            mod_spec(shift_blk), mod_spec(scale_blk), mod_spec(gate_blk),
            vec, vec,
            pl.BlockSpec((None, None, d, th), lambda i, j: (layer, j, 0, 0)),
            pl.BlockSpec((None, th, d), lambda i, j: (layer, j, 0)),
        ],
        out_specs=pl.BlockSpec((tm, d), lambda i, j: (i, 0)),
        out_shape=jax.ShapeDtypeStruct((m, d), F32),
        scratch_shapes=[pltpu.VMEM((tm, d), BF16)],
        compiler_params=_params("parallel", "arbitrary"),
        name="mlp",
    )(x, mod3, mod3, mod3, ln_g, ln_b, w_up, w_down)


def _column_tiles(w, width):
    layers, d, n = w.shape
    return w.reshape(layers, d, n // width, width).transpose(0, 2, 1, 3)


def _rope_tables(seq):
    t = jnp.arange(seq)
    row = (t // GRID_W).astype(F32)
    col = (t % GRID_W).astype(F32)
    axis_dim = HEAD_DIM // 2
    freqs = ROPE_THETA ** (-jnp.arange(0, axis_dim, 2, dtype=F32) / axis_dim)
    ar, ac = row[:, None] * freqs, col[:, None] * freqs
    cos = jnp.concatenate([jnp.cos(ar), jnp.cos(ar), jnp.cos(ac), jnp.cos(ac)], axis=-1)
    sin = jnp.concatenate([-jnp.sin(ar), jnp.sin(ar), -jnp.sin(ac), jnp.sin(ac)], axis=-1)
    return cos, sin


def kernel(x, c, ctx, c_ctx, w_mod, b_mod, w_in, rpb, q_gain, k_gain, conv_w, w_branch, w_o, w_up, w_down,
           ln_g, ln_b):
    batch, seq, d = x.shape
    ctx_len = ctx.shape[1]
    depth = w_mod.shape[0]
    assert d == D_MODEL and batch < MOD_ROWS and seq % NA_QBLK == 0 and seq // GRID_W >= NA_K_ROWS
    alpha = float((2 * depth) ** 0.25)
    ctx_row = batch

    c_all = jnp.concatenate([c, c_ctx[None, :], jnp.zeros((MOD_ROWS - batch - 1, d), F32)], axis=0)
    mod3 = _modulation(c_all, w_mod, b_mod).reshape(depth * MOD_ROWS, 1, 6 * d)
    cos, sin = _rope_tables(seq)
    plan = _na_plan(seq // GRID_W)
    bias = _na_bias(rpb.reshape((depth * NA_HEADS,) + rpb.shape[2:]), plan[2])

    wb_b, wo_b, wd_b = (w.astype(BF16) for w in (w_branch, w_o, w_down))
    w_in_b = _column_tiles(w_in, W_IN_TILE).astype(BF16)
    wu_b = _column_tiles(w_up, MLP_TILE).astype(BF16)
    qg = q_gain.reshape(depth, 1, HEAD_DIM)
    kg = k_gain.reshape(depth, 1, HEAD_DIM)
    lg = ln_g.reshape(depth * 2, 1, d)
    lb = ln_b.reshape(depth * 2, 1, d)

    xl = x.reshape(batch * seq, d)
    xc = ctx.reshape(batch * ctx_len, d)
    for l in range(depth):
        with_ctx_out = l < depth - 1

        z = _ln_matmul(xl, mod3, l, 0, 1, w_in_b, Z_ROT, IN_W, seq, None, tm=1024, tn=1536)
        if with_ctx_out:
            zc = _ln_matmul(xc, mod3, l, 0, 1, w_in_b, Z_ROT, IN_W, None, ctx_row, tm=512, tn=1536)
            ctx_cols = Z_COLS
        else:
            zc = _ln_matmul(xc, mod3, l, 0, 1, w_in_b, KV_COLS.start, KV_COLS.width, None, ctx_row, tm=512, tn=512)
            ctx_cols = KV_COLS

        o_na = _na_attention(z, zc, ctx_cols, bias, l, plan, batch, seq, ctx_len)
        o_gqa = _gqa_attention(z, zc, ctx_cols, qg, kg, l, cos, sin, batch, seq, ctx_len)
        m_act = _merge(o_na, o_gqa, z, conv_w, wb_b, l, seq)
        xl = _out_proj(m_act, wo_b, xl, mod3, l, 2, lg, lb, 2 * l, alpha, seq, None)
        xl = _mlp(xl, mod3, l, 3, 4, 5, lg, lb, 2 * l + 1, wu_b, wd_b, alpha, seq, None)

        if with_ctx_out:
            o_na_c = _ctx_attention(zc, W_QA, W_KA, W_VA, False, qg, kg, l, False, batch, ctx_len)
            o_gqa_c = _ctx_attention(zc, W_QB, W_KB, W_VB, True, qg, kg, l, True, batch, ctx_len)
            m_c = _merge(o_na_c, o_gqa_c, zc, conv_w, wb_b, l, ctx_len)
            xc = _out_proj(m_c, wo_b, xc, mod3, l, 2, lg, lb, 2 * l, alpha, None, ctx_row)
            xc = _mlp(xc, mod3, l, 3, 4, 5, lg, lb, 2 * l + 1, wu_b, wd_b, alpha, None, ctx_row)
    return xl.reshape(batch, seq, d)
```

```python
import functools

import numpy as np
import jax
import jax.numpy as jnp
from jax import lax
from jax.experimental import pallas as pl
from jax.experimental.pallas import tpu as pltpu

D_MODEL = 2048
GRID_W = 64
HEAD_DIM = 128
BRANCH_W = D_MODEL // 2
NA_HEADS = BRANCH_W // HEAD_DIM
GQA_HEADS = BRANCH_W // HEAD_DIM
GQA_KV_HEADS = GQA_HEADS // 4
GQA_GROUP = GQA_HEADS // GQA_KV_HEADS
KV_W = GQA_KV_HEADS * HEAD_DIM
CONV_K = 3
MLP_HIDDEN = 4 * D_MODEL
N_BRANCH = 3
WIN_R = 8
WIN_C = 16
ROPE_THETA = 10000.0
NORM_EPS = 1e-6
NEG_INF = -1e30
ATTN_SCALE = HEAD_DIM ** -0.5
LOG2E = 1.4426950408889634
SCORE_SCALE = ATTN_SCALE * LOG2E

W_QA = 0
W_QB = W_QA + BRANCH_W
W_KA = W_QB + BRANCH_W
W_VA = W_KA + BRANCH_W
W_KB = W_VA + BRANCH_W
W_VB = W_KB + KV_W
W_CB = W_VB + KV_W
W_CC = W_CB + BRANCH_W
W_CX = W_CC + BRANCH_W
W_G = W_CX + BRANCH_W
IN_W = W_G + N_BRANCH * D_MODEL
CONV_W = 3 * BRANCH_W
GATE_W = N_BRANCH * D_MODEL
Z_ROT = 1536
W_IN_TILE = 1536
MLP_TILE = 1024


class _Cols:
    def __init__(self, start, width, rot):
        self.start, self.width, self.rot = start, width, rot

    def block(self, w_col, blk):
        off = (w_col - self.start - self.rot) % self.width
        assert off % blk == 0 and self.width % blk == 0
        n, base = self.width // blk, off // blk
        return lambda k: (base + k) % n

    def index(self, w_col, blk):
        off = (w_col - self.start - self.rot) % self.width
        assert off % blk == 0 and off + blk <= self.width
        return off // blk


Z_COLS = _Cols(0, IN_W, Z_ROT)
KV_COLS = _Cols(W_KA, W_CB - W_KA, 0)

MOD_ROWS = 32
NA_Q_ROWS = 4
NA_K_ROWS = 12
NA_QBLK = NA_Q_ROWS * GRID_W
NA_KWIN = NA_K_ROWS * GRID_W
SUBLANES = 8

VMEM_LIMIT_BYTES = 56 * 1024 * 1024
LN_CHUNK = 128

F32 = jnp.float32
BF16 = jnp.bfloat16


def _params(*semantics):
    return pltpu.CompilerParams(dimension_semantics=semantics, vmem_limit_bytes=VMEM_LIMIT_BYTES)


def _resident(block_shape, index_map):
    return pl.BlockSpec(block_shape, index_map, pipeline_mode=pl.Buffered(1))


def _layer_norm(x, eps=NORM_EPS):
    mu = jnp.mean(x, axis=-1, keepdims=True)
    xc = x - mu
    var = jnp.mean(xc * xc, axis=-1, keepdims=True)
    return xc * lax.rsqrt(var + eps)


def _rms_norm(x, gain):
    return x * lax.rsqrt(jnp.mean(x * x, axis=-1, keepdims=True) + NORM_EPS) * gain


def _rope(x, cos, sin_signed):
    lane = lax.broadcasted_iota(jnp.int32, x.shape, 1)
    partner = jnp.where((lane & 32) == 0, pltpu.roll(x, 96, axis=1), pltpu.roll(x, 32, axis=1))
    return x * cos + partner * sin_signed


def _dot(a, b):
    return jnp.dot(a, b, preferred_element_type=F32)


def _dot_nt(a, b):
    return lax.dot_general(a, b, (((1,), (1,)), ((), ())), preferred_element_type=F32)


def _modulate_rows(x_ref, sh, sc1, h_ref, r0, rows):
    for s in range(rows // LN_CHUNK):
        rr = slice(r0 + s * LN_CHUNK, r0 + (s + 1) * LN_CHUNK)
        h_ref[rr, :] = (_layer_norm(x_ref[rr, :]) * sc1 + sh).astype(h_ref.dtype)


def _residual_norm_rows(x_ref, f, gate, g, b, o_ref, r0, alpha):
    gate_a = gate * (1.0 / alpha)
    eps_a = NORM_EPS / (alpha * alpha)
    for s in range(f.shape[0] // LN_CHUNK):
        rr = slice(r0 + s * LN_CHUNK, r0 + (s + 1) * LN_CHUNK)
        y = x_ref[rr, :] + gate_a * f[s * LN_CHUNK:(s + 1) * LN_CHUNK, :]
        o_ref[rr, :] = _layer_norm(y, eps_a) * g + b


def _mod_row_map(layer, rows_per_mod, tm, fixed_row):
    base = layer * MOD_ROWS
    if fixed_row is not None:
        return lambda i: base + fixed_row
    return lambda i: base + (i * tm) // rows_per_mod


def _mod_kernel(c_ref, w_ref, b_ref, o_ref):
    c = c_ref[...]
    a = (c * jax.nn.sigmoid(c)).astype(BF16)
    o_ref[...] = _dot(a, w_ref[...].astype(BF16)) + b_ref[...]


def _modulation(c_all, w_mod, b_mod, tn=1024):
    depth, d, n = w_mod.shape
    return pl.pallas_call(
        _mod_kernel,
        grid=(depth, n // tn),
        in_specs=[
            pl.BlockSpec((MOD_ROWS, d), lambda l, j: (0, 0)),
            pl.BlockSpec((None, d, tn), lambda l, j: (l, 0, j)),
            pl.BlockSpec((None, 1, tn), lambda l, j: (l, 0, j)),
        ],
        out_specs=pl.BlockSpec((None, MOD_ROWS, tn), lambda l, j: (l, 0, j)),
        out_shape=jax.ShapeDtypeStruct((depth, MOD_ROWS, n), F32),
        compiler_params=_params("parallel", "parallel"),
        name="modulation",
    )(c_all, w_mod, b_mod.reshape(depth, 1, n))


def _ln_matmul_kernel(x_ref, sh_ref, sc_ref, w_ref, o_ref, h_ref, *, rows):
    j = pl.program_id(1)

    @pl.when(j == 0)
    def _():
        sc1 = 1.0 + sc_ref[...]
        sh = sh_ref[...]
        for c in range(x_ref.shape[0] // rows):
            _modulate_rows(x_ref, sh, sc1, h_ref, c * rows, rows)
            r = slice(c * rows, (c + 1) * rows)
            o_ref[r, :] = _dot(h_ref[r, :], w_ref[...]).astype(o_ref.dtype)

    @pl.when(j > 0)
    def _():
        o_ref[...] = _dot(h_ref[...], w_ref[...]).astype(o_ref.dtype)


def _ln_matmul(x, mod3, layer, shift_blk, scale_blk, w, first_col, n, rows_per_mod, fixed_row, tm, tn, rows=256):
    m, d = x.shape
    w_tiles = w.shape[2] // tn
    assert first_col % tn == 0 and n % tn == 0 and w.shape[2] % tn == 0
    row = _mod_row_map(layer, rows_per_mod, tm, fixed_row)
    return pl.pallas_call(
        functools.partial(_ln_matmul_kernel, rows=rows),
        grid=(m // tm, n // tn),
        in_specs=[
            pl.BlockSpec((tm, d), lambda i, j: (i, 0)),
            pl.BlockSpec((None, 1, d), lambda i, j: (row(i), 0, shift_blk)),
            pl.BlockSpec((None, 1, d), lambda i, j: (row(i), 0, scale_blk)),
            pl.BlockSpec((None, d, tn), lambda i, j: (layer, 0, (first_col // tn + j) % w_tiles)),
        ],
        out_specs=pl.BlockSpec((tm, tn), lambda i, j: (i, j)),
        out_shape=jax.ShapeDtypeStruct((m, n), BF16),
        scratch_shapes=[pltpu.VMEM((tm, d), BF16)],
        compiler_params=_params("parallel", "arbitrary"),
        name="ln_matmul",
    )(x, mod3, mod3, w)


def _na_plan(n_rows):
    wr = min(WIN_R, n_rows)
    n_qblk = n_rows // NA_Q_ROWS
    r = np.arange(n_rows)
    row_start = np.clip(r - wr // 2, 0, n_rows - wr)
    starts, pat_ids, patterns, keys = [], [], [], {}
    for qi in range(n_qblk):
        k0 = int(np.clip(qi * NA_Q_ROWS - NA_Q_ROWS, 0, n_rows - NA_K_ROWS))
        q_rows = qi * NA_Q_ROWS + np.arange(NA_Q_ROWS)
        k_rows = k0 + np.arange(NA_K_ROWS)
        valid = (k_rows[None, :] >= row_start[q_rows][:, None]) & (k_rows[None, :] < row_start[q_rows][:, None] + wr)
        assert valid.sum(axis=1).min() == wr, "key window must cover every query row's neighbourhood"
        dr_idx = np.clip(k_rows[None, :] - q_rows[:, None] + WIN_R - 1, 0, 2 * WIN_R - 2)
        key = (valid.tobytes(), dr_idx.tobytes())
        if key not in keys:
            keys[key] = len(patterns)
            patterns.append((valid, dr_idx))
        starts.append(k0)
        pat_ids.append(keys[key])
    return starts, pat_ids, patterns


def _na_bias(rpb, patterns):
    n, n_dr, n_dc = rpb.shape
    col = np.arange(GRID_W)
    col_start = np.clip(col - WIN_C // 2, 0, GRID_W - WIN_C)
    in_win = (col[None, :] >= col_start[:, None]) & (col[None, :] < col_start[:, None] + WIN_C)
    assert np.abs(col[None, :] - col[:, None])[in_win].max() <= WIN_C - 1
    line = 2 * GRID_W
    left = GRID_W - 1 - (WIN_C - 1)
    w = jnp.pad(rpb.astype(F32), ((0, 0), (0, 0), (left, line - left - n_dc)))
    w = jnp.broadcast_to(w[:, :, None, :], (n, n_dr, GRID_W, line)).reshape(n, n_dr, GRID_W * line)
    toe = w[:, :, :GRID_W * (line - 1)].reshape(n, n_dr, GRID_W, line - 1)[..., GRID_W - 1:]
    toe = jnp.where(in_win[None, None], toe * LOG2E, NEG_INF)
    toe = toe.transpose(0, 2, 1, 3).reshape(n, GRID_W, n_dr * GRID_W)
    out = []
    for valid, dr_idx in patterns:
        per_row = []
        for qr in range(NA_Q_ROWS):
            kr = np.flatnonzero(valid[qr])
            a, b = int(kr[0]), int(kr[-1]) + 1
            d0 = int(dr_idx[qr, a])
            assert b - a == len(kr) and (dr_idx[qr, a:b] == d0 + np.arange(b - a)).all()
            piece = toe[:, :, d0 * GRID_W:(d0 + b - a) * GRID_W]
            per_row.append(jnp.pad(piece, ((0, 0), (0, 0), (a * GRID_W, (NA_K_ROWS - b) * GRID_W)),
                                   constant_values=NEG_INF))
        out.append(jnp.stack(per_row, axis=1).reshape(n, NA_QBLK, NA_KWIN))
    return jnp.stack(out, axis=1)


def _fill_values_with_ones(vs_ref, v, vc, seq):
    vs_ref[0:seq, 0:HEAD_DIM] = v
    vs_ref[seq:, 0:HEAD_DIM] = vc
    vs_ref[:, HEAD_DIM:] = jnp.ones((vs_ref.shape[0], HEAD_DIM), vs_ref.dtype)


def _normalised(o_aug):
    return o_aug[:, :HEAD_DIM] / o_aug[:, HEAD_DIM:]


def _na_kernel(q_ref, k_ref, v_ref, kc_ref, vc_ref, bias_ref, o_ref, vs_ref, *, starts, pat_ids, seq):
    for h in range(q_ref.shape[1] // HEAD_DIM):
        hc = slice(h * HEAD_DIM, (h + 1) * HEAD_DIM)
        vs = vs_ref.at[h]
        _fill_values_with_ones(vs, v_ref[:, hc], vc_ref[:, hc], seq)
        kc = kc_ref[:, hc]
        for qi, (k0, pat) in enumerate(zip(starts, pat_ids)):
            rows = slice(qi * NA_QBLK, (qi + 1) * NA_QBLK)
            keys = slice(k0 * GRID_W, k0 * GRID_W + NA_KWIN)
            q = q_ref[rows, hc]
            s_loc = _dot_nt(q, k_ref[keys, hc]) * SCORE_SCALE + bias_ref[h, pat]
            s_ctx = _dot_nt(q, kc) * SCORE_SCALE
            m = jnp.maximum(jnp.max(s_loc, axis=-1, keepdims=True), jnp.max(s_ctx, axis=-1, keepdims=True))
            p_loc = jnp.exp2(s_loc - m).astype(BF16)
            p_ctx = jnp.exp2(s_ctx - m).astype(BF16)
            o_aug = _dot(p_loc, vs[keys, :]) + _dot(p_ctx, vs[seq:, :])
            o_ref[rows, hc] = _normalised(o_aug).astype(o_ref.dtype)


def _na_attention(z, zc, ctx_cols, bias, layer, plan, batch, seq, ctx_len, heads=4):
    starts, pat_ids, patterns = plan
    hw = heads * HEAD_DIM
    steps = NA_HEADS // heads
    qa, ka, va = (Z_COLS.block(c, hw) for c in (W_QA, W_KA, W_VA))
    kc, vc = (ctx_cols.block(c, hw) for c in (W_KA, W_VA))
    kernel = functools.partial(_na_kernel, starts=tuple(starts), pat_ids=tuple(pat_ids), seq=seq)
    return pl.pallas_call(
        kernel,
        grid=(steps, batch),
        in_specs=[
            pl.BlockSpec((seq, hw), lambda h, b: (b, qa(h))),
            pl.BlockSpec((seq, hw), lambda h, b: (b, ka(h))),
            pl.BlockSpec((seq, hw), lambda h, b: (b, va(h))),
            pl.BlockSpec((ctx_len, hw), lambda h, b: (b, kc(h))),
            pl.BlockSpec((ctx_len, hw), lambda h, b: (b, vc(h))),
            pl.BlockSpec((heads, len(patterns), NA_QBLK, NA_KWIN), lambda h, b: (layer * steps + h, 0, 0, 0)),
        ],
        out_specs=pl.BlockSpec((seq, hw), lambda h, b: (b, h)),
        out_shape=jax.ShapeDtypeStruct((batch * seq, BRANCH_W), BF16),
        scratch_shapes=[pltpu.VMEM((heads, seq + ctx_len, 2 * HEAD_DIM), BF16)],
        compiler_params=_params("parallel", "parallel"),
        name="na_attention",
    )(z, z, z, zc, zc, bias)


def _gqa_kernel(q_ref, k_ref, v_ref, kc_ref, vc_ref, qg_ref, kg_ref, cos_ref, sin_ref, o_ref,
                ks_ref, vs_ref, *, tq, seq, qrows):
    qi = pl.program_id(2)

    @pl.when(qi == 0)
    def _():
        kg = kg_ref[...]
        kn = _rms_norm(k_ref[...].astype(F32), kg)
        ks_ref[0:seq, :] = _rope(kn, cos_ref[...], sin_ref[...]).astype(BF16)
        ks_ref[seq:, :] = _rms_norm(kc_ref[...].astype(F32), kg).astype(BF16)
        _fill_values_with_ones(vs_ref, v_ref[...], vc_ref[...], seq)

    qg = qg_ref[...]
    for c in range(tq // qrows):
        r = pl.multiple_of(qi * tq + c * qrows, qrows)
        cos = cos_ref[pl.ds(r, qrows), :]
        sin = sin_ref[pl.ds(r, qrows), :]
        rows = slice(c * qrows, (c + 1) * qrows)
        for g in range(GQA_GROUP):
            cols = slice(g * HEAD_DIM, (g + 1) * HEAD_DIM)
            q = q_ref[rows, cols].astype(F32)
            qn = (_rope(_rms_norm(q, qg), cos, sin) * SCORE_SCALE).astype(BF16)
            s = _dot_nt(qn, ks_ref[...])
            m = jnp.max(s, axis=-1, keepdims=True)
            p = jnp.exp2(s - m).astype(BF16)
            o_ref[rows, cols] = _normalised(_dot(p, vs_ref[...])).astype(o_ref.dtype)


def _gqa_attention(z, zc, ctx_cols, q_gain, k_gain, layer, cos, sin, batch, seq, ctx_len, tq=None, qrows=256):
    hb = HEAD_DIM
    gw = GQA_GROUP * HEAD_DIM
    tq = seq if tq is None else tq
    nq = seq // tq
    qb = Z_COLS.block(W_QB, gw)
    kb, vb = (Z_COLS.block(c, hb) for c in (W_KB, W_VB))
    kc, vc = (ctx_cols.block(c, hb) for c in (W_KB, W_VB))
    kernel = functools.partial(_gqa_kernel, tq=tq, seq=seq, qrows=qrows)
    gain = pl.BlockSpec((None, 1, hb), lambda b, kh, qi: (layer, 0, 0))
    return pl.pallas_call(
        kernel,
        grid=(batch, GQA_KV_HEADS, nq),
        in_specs=[
            pl.BlockSpec((tq, gw), lambda b, kh, qi: (b * nq + qi, qb(kh))),
            pl.BlockSpec((seq, hb), lambda b, kh, qi: (b, kb(kh))),
            pl.BlockSpec((seq, hb), lambda b, kh, qi: (b, vb(kh))),
            pl.BlockSpec((ctx_len, hb), lambda b, kh, qi: (b, kc(kh))),
            pl.BlockSpec((ctx_len, hb), lambda b, kh, qi: (b, vc(kh))),
            gain, gain,
            pl.BlockSpec((seq, hb), lambda b, kh, qi: (0, 0)),
            pl.BlockSpec((seq, hb), lambda b, kh, qi: (0, 0)),
        ],
        out_specs=pl.BlockSpec((tq, gw), lambda b, kh, qi: (b * nq + qi, kh)),
        out_shape=jax.ShapeDtypeStruct((batch * seq, BRANCH_W), BF16),
        scratch_shapes=[pltpu.VMEM((seq + ctx_len, hb), BF16), pltpu.VMEM((seq + ctx_len, 2 * hb), BF16)],
        compiler_params=_params("parallel", "parallel", "arbitrary"),
        name="gqa_attention",
    )(z, z, z, zc, zc, q_gain, k_gain, cos, sin)


def _ctx_attn_kernel(q_ref, k_ref, v_ref, qg_ref, kg_ref, o_ref, *, qk_norm):
    shared = k_ref.shape[1] == HEAD_DIM
    for h in range(q_ref.shape[1] // HEAD_DIM):
        hc = slice(h * HEAD_DIM, (h + 1) * HEAD_DIM)
        kvc = slice(0, HEAD_DIM) if shared else hc
        q = q_ref[:, hc]
        k = k_ref[:, kvc]
        if qk_norm:
            q = _rms_norm(q.astype(F32), qg_ref[...]).astype(BF16)
            k = _rms_norm(k.astype(F32), kg_ref[...]).astype(BF16)
        s = _dot_nt(q, k) * ATTN_SCALE
        m = jnp.max(s, axis=-1, keepdims=True)
        p = jnp.exp(s - m)
        den = jnp.sum(p, axis=-1, keepdims=True)
        o_ref[:, hc] = (_dot(p.astype(BF16), v_ref[:, kvc]) / den).astype(o_ref.dtype)


def _ctx_attention(zc, w_q, w_k, w_v, shared_kv, q_gain, k_gain, layer, qk_norm, batch, ctx_len):
    gw = GQA_GROUP * HEAD_DIM
    kvw = HEAD_DIM if shared_kv else gw
    qc = Z_COLS.block(w_q, gw)
    kc, vc = (Z_COLS.block(c, kvw) for c in (w_k, w_v))
    kernel = functools.partial(_ctx_attn_kernel, qk_norm=qk_norm)
    gain = pl.BlockSpec((None, 1, HEAD_DIM), lambda b, g: (layer, 0, 0))
    return pl.pallas_call(
        kernel,
        grid=(batch, BRANCH_W // gw),
        in_specs=[
            pl.BlockSpec((ctx_len, gw), lambda b, g: (b, qc(g))),
            pl.BlockSpec((ctx_len, kvw), lambda b, g: (b, kc(g))),
            pl.BlockSpec((ctx_len, kvw), lambda b, g: (b, vc(g))),
            gain, gain,
        ],
        out_specs=pl.BlockSpec((ctx_len, gw), lambda b, g: (b, g)),
        out_shape=jax.ShapeDtypeStruct((batch * ctx_len, BRANCH_W), BF16),
        compiler_params=_params("parallel", "parallel"),
        name="ctx_attention",
    )(zc, zc, zc, q_gain, k_gain)


def _conv_tile(c_ref, prev_ref, next_ref, w_ref, *, seq_len, tc):
    tm = c_ref.shape[0]
    chunks = []
    r0 = pl.program_id(0) * tm
    t = (r0 + lax.broadcasted_iota(jnp.int32, (tm, 1), 0)) & (seq_len - 1)
    row = lax.broadcasted_iota(jnp.int32, (tm, 1), 0)
    at_start, at_end = t == 0, t == seq_len - 1
    w = w_ref[...]
    for j in range(BRANCH_W // tc):
        cb = slice(j * tc, (j + 1) * tc)
        cc = slice(BRANCH_W + j * tc, BRANCH_W + (j + 1) * tc)
        cx = slice(2 * BRANCH_W + j * tc, 2 * BRANCH_W + (j + 1) * tc)
        v = c_ref[:, cc].astype(F32) * c_ref[:, cx].astype(F32)
        above = prev_ref[SUBLANES - 1:SUBLANES, cc].astype(F32) * prev_ref[SUBLANES - 1:SUBLANES, cx].astype(F32)
        below = next_ref[0:1, cc].astype(F32) * next_ref[0:1, cx].astype(F32)
        prev = jnp.where(row == 0, above, pltpu.roll(v, 1, axis=0))
        nxt = jnp.where(row == tm - 1, below, pltpu.roll(v, tm - 1, axis=0))
        prev = jnp.where(at_start, 0.0, prev)
        nxt = jnp.where(at_end, 0.0, nxt)
        y = w[0:1, cb] * prev + w[1:2, cb] * v + w[2:3, cb] * nxt
        chunks.append((c_ref[:, cb].astype(F32) * y).astype(BF16))
    return jnp.concatenate(chunks, axis=1)


def _merge_kernel(a0_ref, a1_ref, c_ref, prev_ref, next_ref, cw_ref, g_ref, w_ref, o_ref, *, tn, seq_len, tc):
    d = o_ref.shape[1]
    branches = (a0_ref[...], a1_ref[...], _conv_tile(c_ref, prev_ref, next_ref, cw_ref, seq_len=seq_len, tc=tc))
    for j in range(d // tn):
        cols = slice(j * tn, (j + 1) * tn)
        acc = None
        for k, a in enumerate(branches):
            gate = jax.nn.sigmoid(g_ref[:, k * d + j * tn:k * d + (j + 1) * tn].astype(F32))
            term = gate * _dot(a, w_ref[k, :, cols])
            acc = term if acc is None else acc + term
        o_ref[:, cols] = acc.astype(o_ref.dtype)


def _merge(o_na, o_gqa, z, conv_w, w_branch, layer, seq_len, tm=512, tn=256, tc=256):
    m = o_na.shape[0]
    assert seq_len & (seq_len - 1) == 0 and (tm % seq_len == 0 or seq_len % tm == 0) and m % seq_len == 0
    conv_blk, gate_blk = Z_COLS.index(W_CB, CONV_W), Z_COLS.index(W_G, GATE_W)
    halo = tm // SUBLANES
    last_halo = m // SUBLANES - 1
    act = pl.BlockSpec((tm, BRANCH_W), lambda i: (i, 0))
    return pl.pallas_call(
        functools.partial(_merge_kernel, tn=tn, seq_len=seq_len, tc=tc),
        grid=(m // tm,),
        in_specs=[act, act,
                  pl.BlockSpec((tm, CONV_W), lambda i: (i, conv_blk)),
                  pl.BlockSpec((SUBLANES, CONV_W), lambda i: (jnp.maximum(i * halo - 1, 0), conv_blk)),
                  pl.BlockSpec((SUBLANES, CONV_W), lambda i: (jnp.minimum((i + 1) * halo, last_halo), conv_blk)),
                  pl.BlockSpec((None, CONV_K, BRANCH_W), lambda i: (layer, 0, 0)),
                  pl.BlockSpec((tm, GATE_W), lambda i: (i, gate_blk)),
                  _resident((None, N_BRANCH, BRANCH_W, D_MODEL), lambda i: (layer, 0, 0, 0))],
        out_specs=pl.BlockSpec((tm, D_MODEL), lambda i: (i, 0)),
        out_shape=jax.ShapeDtypeStruct((m, D_MODEL), BF16),
        compiler_params=_params("parallel"),
        name="merge",
    )(o_na, o_gqa, z, z, z, conv_w, z, w_branch)


def _out_proj_kernel(m_ref, w_ref, x_ref, gate_ref, g_ref, b_ref, o_ref, *, alpha, rows):
    gate = gate_ref[...]
    g = g_ref[...]
    b = b_ref[...]
    for c in range(o_ref.shape[0] // rows):
        f = _dot(m_ref[c * rows:(c + 1) * rows, :], w_ref[...])
        _residual_norm_rows(x_ref, f, gate, g, b, o_ref, c * rows, alpha)


def _out_proj(m_act, w_o, x, mod3, layer, gate_blk, ln_g, ln_b, ln_row, alpha, rows_per_mod, fixed_row,
              tm=512, rows=256):
    m, d = x.shape
    row = _mod_row_map(layer, rows_per_mod, tm, fixed_row)
    vec = pl.BlockSpec((None, 1, d), lambda i: (ln_row, 0, 0))
    return pl.pallas_call(
        functools.partial(_out_proj_kernel, alpha=alpha, rows=rows),
        grid=(m // tm,),
        in_specs=[
            pl.BlockSpec((tm, d), lambda i: (i, 0)),
            _resident((None, d, d), lambda i: (layer, 0, 0)),
            pl.BlockSpec((tm, d), lambda i: (i, 0)),
            pl.BlockSpec((None, 1, d), lambda i: (row(i), 0, gate_blk)),
            vec, vec,
        ],
        out_specs=pl.BlockSpec((tm, d), lambda i: (i, 0)),
        out_shape=jax.ShapeDtypeStruct((m, d), F32),
        compiler_params=_params("parallel"),
        name="out_proj",
    )(m_act, w_o, x, mod3, ln_g, ln_b)


def _mlp_kernel(x_ref, sh_ref, sc_ref, gate_ref, g_ref, b_ref, wu_ref, wd_ref, o_ref, h_ref, *, alpha, rows):
    j = pl.program_id(1)
    last = pl.num_programs(1) - 1
    n_chunks = o_ref.shape[0] // rows

    def down(r):
        u = jnp.maximum(_dot(h_ref[r, :], wu_ref[...]), 0.0)
        return _dot((u * u).astype(BF16), wd_ref[...])

    @pl.when(j == 0)
    def _():
        sc1 = 1.0 + sc_ref[...]
        sh = sh_ref[...]
        for c in range(n_chunks):
            _modulate_rows(x_ref, sh, sc1, h_ref, c * rows, rows)
            r = slice(c * rows, (c + 1) * rows)
            o_ref[r, :] = down(r)

    @pl.when(jnp.logical_and(j > 0, j < last))
    def _():
        o_ref[...] += down(slice(None))

    @pl.when(j == last)
    def _():
        gate = gate_ref[...]
        g = g_ref[...]
        b = b_ref[...]
        for c in range(n_chunks):
            r = slice(c * rows, (c + 1) * rows)
            _residual_norm_rows(x_ref, o_ref[r, :] + down(r), gate, g, b, o_ref, c * rows, alpha)


def _mlp(x, mod3, layer, shift_blk, scale_blk, gate_blk, ln_g, ln_b, ln_row, w_up, w_down, alpha,
         rows_per_mod, fixed_row, tm=512, th=MLP_TILE, rows=256):
    m, d = x.shape
    n_tiles = w_up.shape[2] // th
    assert n_tiles >= 2
    row = _mod_row_map(layer, rows_per_mod, tm, fixed_row)
    mod_spec = lambda blk: pl.BlockSpec((None, 1, d), lambda i, j: (row(i), 0, blk))
    vec = pl.BlockSpec((None, 1, d), lambda i, j: (ln_row, 0, 0))
    return pl.pallas_call(
        functools.partial(_mlp_kernel, alpha=alpha, rows=rows),
        grid=(m // tm, n_tiles),
        in_specs=[
            pl.BlockSpec((tm, d), lambda i, j: (i, 0)),
            mod_spec(shift_blk), mod_spec(scale_blk), mod_spec(gate_blk),
            vec, vec,
            pl.BlockSpec((None, d, th), lambda i, j: (layer, 0, j)),
            pl.BlockSpec((None, th, d), lambda i, j: (layer, j, 0)),
        ],
        out_specs=pl.BlockSpec((tm, d), lambda i, j: (i, 0)),
        out_shape=jax.ShapeDtypeStruct((m, d), F32),
        scratch_shapes=[pltpu.VMEM((tm, d), BF16)],
        compiler_params=_params("parallel", "arbitrary"),
        name="mlp",
    )(x, mod3, mod3, mod3, ln_g, ln_b, w_up, w_down)


def _rope_tables(seq):
    t = jnp.arange(seq)
    row = (t // GRID_W).astype(F32)
    col = (t % GRID_W).astype(F32)
    axis_dim = HEAD_DIM // 2
    freqs = ROPE_THETA ** (-jnp.arange(0, axis_dim, 2, dtype=F32) / axis_dim)
    ar, ac = row[:, None] * freqs, col[:, None] * freqs
    cos = jnp.concatenate([jnp.cos(ar), jnp.cos(ar), jnp.cos(ac), jnp.cos(ac)], axis=-1)
    sin = jnp.concatenate([-jnp.sin(ar), jnp.sin(ar), -jnp.sin(ac), jnp.sin(ac)], axis=-1)
    return cos, sin


def kernel(x, c, ctx, c_ctx, w_mod, b_mod, w_in, rpb, q_gain, k_gain, conv_w, w_branch, w_o, w_up, w_down,
           ln_g, ln_b):
    batch, seq, d = x.shape
    ctx_len = ctx.shape[1]
    depth = w_mod.shape[0]
    assert d == D_MODEL and batch < MOD_ROWS and seq % NA_QBLK == 0 and seq // GRID_W >= NA_K_ROWS
    alpha = float((2 * depth) ** 0.25)
    ctx_row = batch

    c_all = jnp.concatenate([c, c_ctx[None, :], jnp.zeros((MOD_ROWS - batch - 1, d), F32)], axis=0)
    mod3 = _modulation(c_all, w_mod, b_mod).reshape(depth * MOD_ROWS, 1, 6 * d)
    cos, sin = _rope_tables(seq)
    plan = _na_plan(seq // GRID_W)
    bias = _na_bias(rpb.reshape((depth * NA_HEADS,) + rpb.shape[2:]), plan[2])

    w_in_b, wb_b, wo_b, wu_b, wd_b = (w.astype(BF16) for w in (w_in, w_branch, w_o, w_up, w_down))
    qg = q_gain.reshape(depth, 1, HEAD_DIM)
    kg = k_gain.reshape(depth, 1, HEAD_DIM)
    lg = ln_g.reshape(depth * 2, 1, d)
    lb = ln_b.reshape(depth * 2, 1, d)

    xl = x.reshape(batch * seq, d)
    xc = ctx.reshape(batch * ctx_len, d)
    for l in range(depth):
        with_ctx_out = l < depth - 1

        z = _ln_matmul(xl, mod3, l, 0, 1, w_in_b, Z_ROT, IN_W, seq, None, tm=1024, tn=W_IN_TILE)
        if with_ctx_out:
            zc = _ln_matmul(xc, mod3, l, 0, 1, w_in_b, Z_ROT, IN_W, None, ctx_row, tm=512, tn=W_IN_TILE)
            ctx_cols = Z_COLS
        else:
            zc = _ln_matmul(xc, mod3, l, 0, 1, w_in_b, KV_COLS.start, KV_COLS.width, None, ctx_row, tm=512, tn=512)
            ctx_cols = KV_COLS

        o_na = _na_attention(z, zc, ctx_cols, bias, l, plan, batch, seq, ctx_len)
        o_gqa = _gqa_attention(z, zc, ctx_cols, qg, kg, l, cos, sin, batch, seq, ctx_len)
        m_act = _merge(o_na, o_gqa, z, conv_w, wb_b, l, seq)
        xl = _out_proj(m_act, wo_b, xl, mod3, l, 2, lg, lb, 2 * l, alpha, seq, None)
        xl = _mlp(xl, mod3, l, 3, 4, 5, lg, lb, 2 * l + 1, wu_b, wd_b, alpha, seq, None)

        if with_ctx_out:
            o_na_c = _ctx_attention(zc, W_QA, W_KA, W_VA, False, qg, kg, l, False, batch, ctx_len)
            o_gqa_c = _ctx_attention(zc, W_QB, W_KB, W_VB, True, qg, kg, l, True, batch, ctx_len)
            m_c = _merge(o_na_c, o_gqa_c, zc, conv_w, wb_b, l, ctx_len)
            xc = _out_proj(m_c, wo_b, xc, mod3, l, 2, lg, lb, 2 * l, alpha, None, ctx_row)
            xc = _mlp(xc, mod3, l, 3, 4, 5, lg, lb, 2 * l + 1, wu_b, wd_b, alpha, None, ctx_row)
    return xl.reshape(batch, seq, d)
```
